```python
import math
import jax, jax.numpy as jnp
from jax import lax
import numpy as np


D_MODEL = 1024
BATCH = 4
SEQ = 8192
DEPTH = 2

HEAD_DIM = 64
GROUP_HEADS = 4
GROUP_WIDTH = GROUP_HEADS * HEAD_DIM
N_MIXERS = 4
D_MIX = N_MIXERS * GROUP_WIDTH
D_FF = 4 * D_MODEL
NORM_EPS = 1e-6

NUM_BUCKETS = 32
MAX_DISTANCE = 2048
N_BIAS_HEADS = 2 * GROUP_HEADS

DILATED_PATTERNS = ((128, 1), (512, 4), (2048, 16))
DIL_BLOCK = 128

LRU_WIDTH = GROUP_WIDTH
LRU_BLOCKS = GROUP_HEADS
LRU_BLOCK_DIM = LRU_WIDTH // LRU_BLOCKS
LRU_C = 8.0
CONV_WIDTH = 4

SSM_HEADS = GROUP_HEADS
SSM_HEAD_DIM = HEAD_DIM
SSM_INNER = SSM_HEADS * SSM_HEAD_DIM
SSM_GROUPS = 2
SSM_STATE = 128
SSM_CHUNK = 128
SSM_CONV_DIM = SSM_INNER + 2 * SSM_GROUPS * SSM_STATE

DIFF_HEADS = GROUP_HEADS
DIFF_V_DIM = HEAD_DIM
DIFF_QK_DIM = HEAD_DIM // 2
DIFF_BLOCK = 128

A_COLS = 3 * GROUP_WIDTH
B_COLS = 2 * LRU_WIDTH
C_COLS = SSM_INNER + SSM_CONV_DIM + SSM_HEADS
D_COLS = 3 * GROUP_WIDTH
P_IN = A_COLS + B_COLS + C_COLS + D_COLS

kernel_name = 'hybrid_parallel_heads_dilated_lru_ssd_diff'


def rms_norm(x, g, eps=NORM_EPS):
    xf = x.astype(jnp.float32)
    y = xf * lax.rsqrt(jnp.mean(xf * xf, axis=-1, keepdims=True) + eps)
    return (y * g.astype(jnp.float32)).astype(x.dtype)


def t5_bucket(dist):
    n = jnp.maximum(dist, 0)
    max_exact = NUM_BUCKETS // 2
    nf = jnp.maximum(n, 1).astype(jnp.float32)
    large = max_exact + (jnp.log(nf / max_exact) / math.log(MAX_DISTANCE / max_exact)
                         * (NUM_BUCKETS - max_exact)).astype(jnp.int32)
    large = jnp.minimum(large, NUM_BUCKETS - 1)
    return jnp.where(n < max_exact, n, large)


def causal_depthwise_conv(x, w, b):
    k = w.shape[0]
    y = lax.conv_general_dilated(x, w[:, None, :].astype(x.dtype), window_strides=(1,),
                                 padding=((k - 1, 0),), dimension_numbers=('NWC', 'WIO', 'NWC'),
                                 feature_group_count=x.shape[-1])
    return y + b.astype(x.dtype)


def dilated_window_attention(q, k, v, bias_table):
    bsz, s, h, dh = q.shape
    scale = dh ** -0.5
    qi = jnp.arange(DIL_BLOCK)[:, None]
    kj = jnp.arange(2 * DIL_BLOCK)[None, :]
    rel = qi + DIL_BLOCK - kj
    outs, lses = [], []
    for window, dil in DILATED_PATTERNS:
        span = window // dil
        L = s // dil
        nb = -(-L // DIL_BLOCK)
        Lp = nb * DIL_BLOCK

        def strided(t):
            t = t.reshape(bsz, L, dil, h, dh).transpose(0, 3, 2, 1, 4)
            t = jnp.pad(t, ((0, 0), (0, 0), (0, 0), (0, Lp - L), (0, 0)))
            return t.reshape(bsz, h, dil, nb, DIL_BLOCK, dh)

        def with_prev(t):
            prev = jnp.pad(t, ((0, 0), (0, 0), (0, 0), (1, 0), (0, 0), (0, 0)))[:, :, :, :-1]
            return jnp.concatenate([prev, t], axis=-2)

        qb = strided(q)
        kc = with_prev(strided(k))
        vc = with_prev(strided(v)).astype(jnp.float32)
        blk = jnp.arange(nb)[:, None, None]
        valid = (rel >= 0) & (rel <= span) & (blk * DIL_BLOCK + kj - DIL_BLOCK >= 0)
        bias = bias_table[t5_bucket(rel * dil)].astype(jnp.float32)
        bias = jnp.moveaxis(bias, -1, 0)[:, None, None]
        sc = jnp.einsum('bhrnqd,bhrnkd->bhrnqk', qb, kc).astype(jnp.float32) * scale + bias
        sc = jnp.where(valid, sc, -jnp.inf)
        m = jnp.max(sc, axis=-1, keepdims=True)
        e = jnp.exp(sc - m)
        den = jnp.sum(e, axis=-1, keepdims=True)
        o = jnp.einsum('bhrnqk,bhrnkd->bhrnqd', e, vc) / den
        lse = (m + jnp.log(den))[..., 0]
        o = o.reshape(bsz, h, dil, Lp, dh)[:, :, :, :L].transpose(0, 1, 3, 2, 4).reshape(bsz, h, s, dh)
        lse = lse.reshape(bsz, h, dil, Lp)[..., :L].transpose(0, 1, 3, 2).reshape(bsz, h, s)
        outs.append(o)
        lses.append(lse)
    wts = jax.nn.softmax(jnp.stack(lses), axis=0)
    out = jnp.sum(wts[..., None] * jnp.stack(outs), axis=0)
    return out.transpose(0, 2, 1, 3).reshape(bsz, s, h * dh).astype(q.dtype)


def rg_lru(xg, xr, conv_w, conv_b, wa, ba, wx, bx, lam):
    bsz, s, _ = xr.shape
    f32 = jnp.float32
    xc = causal_depthwise_conv(xr, conv_w, conv_b).astype(f32)
    xh = xc.reshape(bsz, s, LRU_BLOCKS, LRU_BLOCK_DIM)
    r = jax.nn.sigmoid(jnp.einsum('bshi,hij->bshj', xh, wa.astype(f32)).reshape(bsz, s, LRU_WIDTH) + ba)
    i = jax.nn.sigmoid(jnp.einsum('bshi,hij->bshj', xh, wx.astype(f32)).reshape(bsz, s, LRU_WIDTH) + bx)
    log_a = -LRU_C * r * jax.nn.softplus(-lam.astype(f32))
    a = jnp.exp(log_a)
    b = jnp.sqrt(-jnp.expm1(2.0 * log_a)) * (i * xc)

    def combine(left, right):
        return (left[0] * right[0], right[0] * left[1] + right[1])

    _, hs = lax.associative_scan(combine, (a, b), axis=1)
    return (jax.nn.gelu(xg.astype(f32), approximate=True) * hs).astype(xr.dtype)


def segsum(a):
    t = a.shape[-1]
    cs = jnp.cumsum(a, axis=-1)
    diff = cs[..., :, None] - cs[..., None, :]
    mask = jnp.tril(jnp.ones((t, t), dtype=bool))
    return jnp.where(mask, diff, -jnp.inf)


def mamba2_ssd(z, xbc, dt, conv_w, conv_b, dt_bias, a_log, d_skip, norm_gain):
    bsz, s, _ = z.shape
    f32 = jnp.float32
    xbc = jax.nn.silu(causal_depthwise_conv(xbc, conv_w, conv_b).astype(f32))
    xs, bm, cm = jnp.split(xbc, [SSM_INNER, SSM_INNER + SSM_GROUPS * SSM_STATE], axis=-1)
    dt = jax.nn.softplus(dt.astype(f32) + dt_bias)
    a = -jnp.exp(a_log.astype(f32))
    nc = s // SSM_CHUNK
    rh = SSM_HEADS // SSM_GROUPS
    xh = xs.reshape(bsz, nc, SSM_CHUNK, SSM_GROUPS, rh, SSM_HEAD_DIM)
    dth = dt.reshape(bsz, nc, SSM_CHUNK, SSM_GROUPS, rh)
    xdt = xh * dth[..., None]
    bm = bm.reshape(bsz, nc, SSM_CHUNK, SSM_GROUPS, SSM_STATE)
    cm = cm.reshape(bsz, nc, SSM_CHUNK, SSM_GROUPS, SSM_STATE)
    adt = (dth * a.reshape(SSM_GROUPS, rh)).transpose(0, 3, 4, 1, 2)
    a_cum = jnp.cumsum(adt, axis=-1)
    lmat = jnp.exp(segsum(adt))
    cb = jnp.einsum('bclgn,bcsgn->bcgls', cm, bm)
    y_diag = jnp.einsum('bcgls,bgrcls,bcsgrp->bclgrp', cb, lmat, xdt)
    decay_states = jnp.exp(a_cum[..., -1:] - a_cum)
    states = jnp.einsum('bcsgn,bgrcs,bcsgrp->bcgrpn', bm, decay_states, xdt)
    chunk_decay = jnp.exp(a_cum[..., -1])

    def step(carry, inp):
        st, dec = inp
        return carry * dec[..., None, None] + st, carry

    init = jnp.zeros((bsz, SSM_GROUPS, rh, SSM_HEAD_DIM, SSM_STATE), f32)
    _, prev = lax.scan(step, init, (jnp.moveaxis(states, 1, 0), jnp.moveaxis(chunk_decay, -1, 0)))
    prev = jnp.moveaxis(prev, 0, 1)
    y_off = jnp.einsum('bclgn,bcgrpn,bgrcl->bclgrp', cm, prev, jnp.exp(a_cum))
    y = y_diag + y_off + xh * d_skip.astype(f32).reshape(SSM_GROUPS, rh)[..., None]
    y = y.reshape(bsz, s, SSM_INNER) * jax.nn.silu(z.astype(f32))
    y = rms_norm(y.reshape(bsz, s, SSM_GROUPS, SSM_INNER // SSM_GROUPS),
                 norm_gain.reshape(SSM_GROUPS, SSM_INNER // SSM_GROUPS))
    return y.reshape(bsz, s, SSM_INNER).astype(z.dtype)


def diff_attention(q, k, v, lam, lam_init, bias_table, sub_gain):
    bsz, s, h = q.shape[:3]
    scale = DIFF_QK_DIM ** -0.5
    nb = s // DIFF_BLOCK
    qb = q.reshape(bsz, nb, DIFF_BLOCK, h, 2, DIFF_QK_DIM).transpose(1, 0, 3, 4, 2, 5)
    kt = k.transpose(0, 2, 3, 1, 4)
    vt = v.transpose(0, 2, 1, 3).astype(jnp.float32)
    kpos = jnp.arange(s)

    def one_block(args):
        qblk, bi = args
        qpos = bi * DIFF_BLOCK + jnp.arange(DIFF_BLOCK)
        dist = qpos[:, None] - kpos[None, :]
        bias = jnp.moveaxis(bias_table[t5_bucket(dist)].astype(jnp.float32), -1, 0)
        sc = jnp.einsum('bhcqd,bhckd->bhcqk', qblk, kt).astype(jnp.float32) * scale + bias[:, None]
        sc = jnp.where(dist >= 0, sc, -jnp.inf)
        p = jax.nn.softmax(sc, axis=-1)
        attn = p[:, :, 0] - lam * p[:, :, 1]
        return jnp.einsum('bhqk,bhkd->bhqd', attn, vt)

    o = lax.map(one_block, (qb, jnp.arange(nb)))
    o = o.transpose(1, 0, 3, 2, 4).reshape(bsz, s, h, DIFF_V_DIM)
    o = rms_norm(o, sub_gain) * (1.0 - lam_init)
    return o.reshape(bsz, s, h * DIFF_V_DIM).astype(q.dtype)


def setup_inputs(seed: int = 0) -> dict:
    key = jax.random.key(seed)
    ks = jax.random.split(key, 32)
    f32 = jnp.float32

    def nrm(k, shape, scale):
        return jax.random.normal(k, shape, f32) * scale

    def gain(k, shape):
        return 1.0 + 0.05 * jax.random.normal(k, shape, f32)

    u = jax.random.uniform(ks[12], (DEPTH, LRU_WIDTH), f32, 0.9, 0.999)
    a_base = u ** (1.0 / LRU_C)
    lru_lambda = jnp.log(a_base) - jnp.log1p(-a_base)
    dt0 = jnp.exp(jax.random.uniform(ks[15], (DEPTH, SSM_HEADS), f32, math.log(1e-3), math.log(1e-1)))
    ssm_dt_bias = dt0 + jnp.log(-jnp.expm1(-dt0))
    return {
        'x': nrm(ks[0], (BATCH, SEQ, D_MODEL), 1.0),
        'rel_bias': nrm(ks[1], (NUM_BUCKETS, N_BIAS_HEADS), 0.5),
        'norm_mix_pre': gain(ks[2], (DEPTH, D_MODEL)),
        'norm_mix_post': gain(ks[3], (DEPTH, D_MODEL)),
        'norm_ffn_pre': gain(ks[4], (DEPTH, D_MODEL)),
        'norm_ffn_post': gain(ks[5], (DEPTH, D_MODEL)),
        'w_in': nrm(ks[6], (DEPTH, D_MODEL, P_IN), D_MODEL ** -0.5),
        'w_out': nrm(ks[7], (DEPTH, D_MIX, D_MODEL), D_MIX ** -0.5),
        'lru_conv_w': nrm(ks[8], (DEPTH, CONV_WIDTH, LRU_WIDTH), CONV_WIDTH ** -0.5),
        'lru_conv_b': nrm(ks[9], (DEPTH, LRU_WIDTH), 0.02),
        'lru_wa': nrm(ks[10], (DEPTH, LRU_BLOCKS, LRU_BLOCK_DIM, LRU_BLOCK_DIM), LRU_BLOCK_DIM ** -0.5),
        'lru_ba': nrm(ks[11], (DEPTH, LRU_WIDTH), 0.02),
        'lru_wx': nrm(ks[13], (DEPTH, LRU_BLOCKS, LRU_BLOCK_DIM, LRU_BLOCK_DIM), LRU_BLOCK_DIM ** -0.5),
        'lru_bx': nrm(ks[14], (DEPTH, LRU_WIDTH), 0.02),
        'lru_lambda': lru_lambda,
        'ssm_conv_w': nrm(ks[16], (DEPTH, CONV_WIDTH, SSM_CONV_DIM), CONV_WIDTH ** -0.5),
        'ssm_conv_b': nrm(ks[17], (DEPTH, SSM_CONV_DIM), 0.02),
        'ssm_dt_bias': ssm_dt_bias,
        'ssm_a_log': jnp.log(jax.random.uniform(ks[18], (DEPTH, SSM_HEADS), f32, 1.0, 16.0)),
        'ssm_d': 1.0 + 0.1 * jax.random.normal(ks[19], (DEPTH, SSM_HEADS), f32),
        'ssm_norm': gain(ks[20], (DEPTH, SSM_INNER)),
        'diff_lq1': nrm(ks[21], (DEPTH, DIFF_QK_DIM), 0.1),
        'diff_lk1': nrm(ks[22], (DEPTH, DIFF_QK_DIM), 0.1),
        'diff_lq2': nrm(ks[23], (DEPTH, DIFF_QK_DIM), 0.1),
        'diff_lk2': nrm(ks[24], (DEPTH, DIFF_QK_DIM), 0.1),
        'diff_norm': gain(ks[25], (DEPTH, DIFF_V_DIM)),
        'w_ff_up': nrm(ks[26], (DEPTH, D_MODEL, D_FF), D_MODEL ** -0.5),
        'w_ff_down': nrm(ks[27], (DEPTH, D_FF, D_MODEL), D_FF ** -0.5),
    }


def reference(x, rel_bias, norm_mix_pre, norm_mix_post, norm_ffn_pre, norm_ffn_post, w_in, w_out,
              lru_conv_w, lru_conv_b, lru_wa, lru_ba, lru_wx, lru_bx, lru_lambda,
              ssm_conv_w, ssm_conv_b, ssm_dt_bias, ssm_a_log, ssm_d, ssm_norm,
              diff_lq1, diff_lk1, diff_lq2, diff_lk2, diff_norm, w_ff_up, w_ff_down):
    bsz, s, _ = x.shape
    h = x
    for layer in range(DEPTH):
        u = rms_norm(h, norm_mix_pre[layer])
        proj = jnp.einsum('bsd,dp->bsp', u, w_in[layer])
        pa, pb, pc, pd = jnp.split(proj, [A_COLS, A_COLS + B_COLS, A_COLS + B_COLS + C_COLS], axis=-1)

        qa, ka, va = (t.reshape(bsz, s, GROUP_HEADS, HEAD_DIM) for t in jnp.split(pa, 3, axis=-1))
        ya = dilated_window_attention(qa, ka, va, rel_bias[:, :GROUP_HEADS])

        gb, xb = jnp.split(pb, 2, axis=-1)
        yb = rg_lru(gb, xb, lru_conv_w[layer], lru_conv_b[layer], lru_wa[layer], lru_ba[layer],
                    lru_wx[layer], lru_bx[layer], lru_lambda[layer])

        zc, xbc, dtc = jnp.split(pc, [SSM_INNER, SSM_INNER + SSM_CONV_DIM], axis=-1)
        yc = mamba2_ssd(zc, xbc, dtc, ssm_conv_w[layer], ssm_conv_b[layer], ssm_dt_bias[layer],
                        ssm_a_log[layer], ssm_d[layer], ssm_norm[layer])

        qd, kd, vd = jnp.split(pd, 3, axis=-1)
        lam_init = 0.8 - 0.6 * math.exp(-0.3 * layer)
        lam = (jnp.exp(jnp.sum(diff_lq1[layer].astype(jnp.float32) * diff_lk1[layer].astype(jnp.float32)))
               - jnp.exp(jnp.sum(diff_lq2[layer].astype(jnp.float32) * diff_lk2[layer].astype(jnp.float32)))
               + lam_init)
        yd = diff_attention(qd.reshape(bsz, s, DIFF_HEADS, 2, DIFF_QK_DIM),
                            kd.reshape(bsz, s, DIFF_HEADS, 2, DIFF_QK_DIM),
                            vd.reshape(bsz, s, DIFF_HEADS, DIFF_V_DIM),
                            lam, lam_init, rel_bias[:, GROUP_HEADS:], diff_norm[layer])

        mix = jnp.concatenate([ya, yb, yc, yd], axis=-1)
        h = h + rms_norm(jnp.einsum('bsm,md->bsd', mix, w_out[layer]), norm_mix_post[layer])

        u = rms_norm(h, norm_ffn_pre[layer])
        f = jnp.square(jax.nn.relu(jnp.einsum('bsd,df->bsf', u, w_ff_up[layer])))
        h = h + rms_norm(jnp.einsum('bsf,fd->bsd', f, w_ff_down[layer]), norm_ffn_post[layer])
    return h
```

```python
import functools
import math

import numpy as np
import jax
import jax.numpy as jnp
from jax import lax
from jax.experimental import pallas as pl
from jax.experimental.pallas import tpu as pltpu

F32 = jnp.float32
BF16 = jnp.bfloat16
HIGHEST = lax.Precision.HIGHEST

NORM_EPS = 1e-6
HEAD_DIM = 64
GROUP_HEADS = 4
GROUP_WIDTH = GROUP_HEADS * HEAD_DIM
NUM_BUCKETS = 32
MAX_DISTANCE = 2048
DILATED_PATTERNS = ((128, 1), (512, 4), (2048, 16))
DIL_BLOCK = 128
LRU_C = 8.0
CONV_WIDTH = 4
SSM_GROUPS = 2
SSM_STATE = 128
SSM_CHUNK = 128
SSM_CONV_DIM = GROUP_WIDTH + 2 * SSM_GROUPS * SSM_STATE
DIFF_QK_DIM = HEAD_DIM // 2
DIFF_TQ = 256
LANE = 128
CONV_PAD = 8
VMEM_LIMIT = 48 * 1024 * 1024

_C_QA, _C_KA, _C_VA = 0, 256, 512
_C_GB, _C_XB = 768, 1024
_C_ZC, _C_XBC = 1280, 1536
_C_QD, _C_KD, _C_VD = 2304, 2560, 2816
_C_DT = 3072
_C_END = 3584


def _t5_thresholds():
    n = np.arange(1, 4 * MAX_DISTANCE)
    max_exact = NUM_BUCKETS // 2
    large = max_exact + (np.log(n / max_exact) / math.log(MAX_DISTANCE / max_exact)
                         * (NUM_BUCKETS - max_exact)).astype(np.int64)
    bucket = np.where(n < max_exact, n, np.minimum(large, NUM_BUCKETS - 1))
    return tuple(int(n[bucket >= b].min()) for b in range(1, NUM_BUCKETS))


_T5_THR = _t5_thresholds()
DIFF_NEAR = -(-(_T5_THR[-1] + DIFF_TQ - 1) // DIFF_TQ)


def _bias_from_dist(dist, tab_ref, col):
    out = jnp.full(dist.shape, tab_ref[0, col], F32)
    for b in range(1, NUM_BUCKETS):
        out = jnp.where(dist >= _T5_THR[b - 1], tab_ref[b, col], out)
    return out


def _rms(x, g):
    return x * lax.rsqrt(jnp.mean(x * x, axis=-1, keepdims=True) + NORM_EPS) * g


def _params(n_axes):
    return pltpu.CompilerParams(dimension_semantics=("arbitrary",) * n_axes,
                                vmem_limit_bytes=VMEM_LIMIT)


def _full(shape):
    return pl.BlockSpec(shape, lambda *_: (0,) * len(shape))


def _in_proj_kernel(h_ref, g_ref, w_ref, qa, ka, va, gb, xb, zc, xbc, qd, kdt, vd, dt):
    u = _rms(h_ref[...], g_ref[...]).astype(BF16)

    def seg(lo, hi):
        return jnp.dot(u, w_ref[:, lo:hi], preferred_element_type=F32)

    qa[...] = (seg(_C_QA, _C_KA) * (HEAD_DIM ** -0.5)).astype(BF16)
    ka[...] = seg(_C_KA, _C_VA).astype(BF16)
    va[...] = seg(_C_VA, _C_GB).astype(BF16)
    gb[...] = seg(_C_GB, _C_XB)
    xb[...] = seg(_C_XB, _C_ZC)
    zc[...] = seg(_C_ZC, _C_XBC)
    xbc[...] = seg(_C_XBC, _C_QD)
    qd[...] = (seg(_C_QD, _C_KD) * (DIFF_QK_DIM ** -0.5)).astype(BF16)
    kd = seg(_C_KD, _C_VD)
    for j in range(kdt.shape[0]):
        kdt[j] = kd[j * DIFF_TQ:(j + 1) * DIFF_TQ, :].T.astype(BF16)
    vd[...] = seg(_C_VD, _C_DT).astype(BF16)
    dt[...] = seg(_C_DT, _C_END)


def _in_proj(h, gain, w_all, tm=512):
    bsz, s, d = h.shape
    gw = GROUP_WIDTH
    row = lambda width: pl.BlockSpec((None, tm, width), lambda b, i: (b, i, 0))
    shp = lambda width, dt: jax.ShapeDtypeStruct((bsz, s, width), dt)
    nkb = tm // DIFF_TQ
    out_shape = (shp(gw, BF16), shp(gw, BF16), shp(gw, BF16), shp(gw, F32), shp(gw, F32),
                 shp(gw, F32), shp(SSM_CONV_DIM, F32), shp(gw, BF16),
                 jax.ShapeDtypeStruct((bsz, s // DIFF_TQ, gw, DIFF_TQ), BF16),
                 shp(gw, BF16), shp(4 * LANE, F32))
    out_specs = (row(gw), row(gw), row(gw), row(gw), row(gw), row(gw), row(SSM_CONV_DIM), row(gw),
                 pl.BlockSpec((None, nkb, gw, DIFF_TQ), lambda b, i: (b, i, 0, 0)),
                 row(gw), row(4 * LANE))
    return pl.pallas_call(
        _in_proj_kernel, grid=(bsz, s // tm),
        in_specs=[row(d), _full((1, d)), _full(w_all.shape)],
        out_specs=out_specs, out_shape=out_shape, compiler_params=_params(2),
    )(h, gain, w_all)


def _dil_kernel(tab_ref, q_ref, kp_ref, kc_ref, vp_ref, vc_ref, o_ref, lse_ref, bias_scr, *, dil, span):
    first_step = (pl.program_id(0) == 0) & (pl.program_id(1) == 0) & (pl.program_id(2) == 0)
    blk = pl.program_id(2)
    nq, nk = DIL_BLOCK, 2 * DIL_BLOCK

    @pl.when(first_step)
    def _():
        qi = lax.broadcasted_iota(jnp.int32, (nq, nk), 0)
        kj = lax.broadcasted_iota(jnp.int32, (nq, nk), 1)
        rel = qi + DIL_BLOCK - kj
        valid = (rel >= 0) & (rel <= span)
        dist = jnp.maximum(rel, 0) * dil
        for h in range(GROUP_HEADS):
            bias_scr[h] = jnp.where(valid, _bias_from_dist(dist, tab_ref, h), -jnp.inf)

    q = q_ref[...]
    k = jnp.concatenate([kp_ref[...], kc_ref[...]], axis=0)
    v = jnp.concatenate([vp_ref[...], vc_ref[...]], axis=0)
    head_of_lane = lax.broadcasted_iota(jnp.int32, (1, GROUP_WIDTH), 1) // HEAD_DIM
    kj = lax.broadcasted_iota(jnp.int32, (1, nk), 1)
    key_ok = kj >= jnp.where(blk > 0, 0, DIL_BLOCK)
    o_acc = jnp.zeros((nq, GROUP_WIDTH), F32)
    lse_acc = jnp.zeros((nq, GROUP_WIDTH), F32)
    for h in range(GROUP_HEADS):
        hm = head_of_lane == h
        qh = jnp.where(hm, q, jnp.zeros_like(q))
        sc = lax.dot_general(qh, k, (((1,), (1,)), ((), ())), preferred_element_type=F32) + bias_scr[h]
        sc = jnp.where(key_ok, sc, -jnp.inf)
        m = jnp.max(sc, axis=-1, keepdims=True)
        e = jnp.exp(sc - m)
        den = jnp.sum(e, axis=-1, keepdims=True)
        oh = jnp.dot(e.astype(BF16), v, preferred_element_type=F32) / den
        o_acc = jnp.where(hm, oh, o_acc)
        lse_acc = jnp.where(hm, m + jnp.log(den), lse_acc)
    o_ref[...] = o_acc
    lse_ref[...] = lse_acc


def _dil_combine_kernel(o1, o2, o3, l1, l2, l3, out):
    a, b, c = l1[...], l2[...], l3[...]
    m = jnp.maximum(jnp.maximum(a, b), c)
    wa, wb, wc = jnp.exp(a - m), jnp.exp(b - m), jnp.exp(c - m)
    out[...] = ((wa * o1[...] + wb * o2[...] + wc * o3[...]) / (wa + wb + wc)).astype(out.dtype)


def _dilated_attention(q, k, v, rel_bias, tm=512):
    bsz, s, gw = q.shape
    outs, lses = [], []
    for window, dil in DILATED_PATTERNS:
        length = s // dil
        assert s % dil == 0 and length % DIL_BLOCK == 0
        nb = length // DIL_BLOCK
        view = lambda t: t.reshape(bsz, length, dil * gw)
        cur = pl.BlockSpec((None, DIL_BLOCK, gw), lambda b, r, n: (b, n, r))
        prev = pl.BlockSpec((None, DIL_BLOCK, gw), lambda b, r, n: (b, jnp.maximum(n - 1, 0), r))
        o, lse = pl.pallas_call(
            functools.partial(_dil_kernel, dil=dil, span=window // dil),
            grid=(bsz, dil, nb),
            in_specs=[pl.BlockSpec(memory_space=pltpu.SMEM), cur, prev, cur, prev, cur],
            out_specs=(cur, cur),
            out_shape=(jax.ShapeDtypeStruct((bsz, length, dil * gw), F32),) * 2,
            scratch_shapes=[pltpu.VMEM((GROUP_HEADS, DIL_BLOCK, 2 * DIL_BLOCK), F32)],
            compiler_params=_params(3),
        )(rel_bias, view(q), view(k), view(k), view(v), view(v))
        outs.append(o.reshape(bsz, s, gw))
        lses.append(lse.reshape(bsz, s, gw))
    row = pl.BlockSpec((None, tm, gw), lambda b, i: (b, i, 0))
    return pl.pallas_call(
        _dil_combine_kernel, grid=(bsz, s // tm), in_specs=[row] * 6, out_specs=row,
        out_shape=jax.ShapeDtypeStruct((bsz, s, gw), BF16), compiler_params=_params(2),
    )(*outs, *lses)


def _causal_conv(x, xbuf, cw_ref, cb_ref, first_tile):
    t = x.shape[0]

    @pl.when(first_tile)
    def _():
        xbuf[0:CONV_PAD, :] = jnp.zeros((CONV_PAD, x.shape[1]), F32)

    xbuf[CONV_PAD:CONV_PAD + t, :] = x
    y = cb_ref[...] + cw_ref[CONV_WIDTH - 1:CONV_WIDTH, :] * x
    for kk in range(CONV_WIDTH - 1):
        off = CONV_PAD - (CONV_WIDTH - 1) + kk
        y = y + cw_ref[kk:kk + 1, :] * xbuf[off:off + t, :]
    xbuf[0:CONV_PAD, :] = x[t - CONV_PAD:t, :]
    return y


def _lru_kernel(g_ref, x_ref, cw_ref, cb_ref, wa_ref, ba_ref, wx_ref, bx_ref, lam_ref, o_ref,
                xbuf, a_scr, b_scr, hcar):
    first_tile = pl.program_id(1) == 0
    ts = x_ref.shape[0]

    @pl.when(first_tile)
    def _():
        hcar[...] = jnp.zeros_like(hcar)

    xc = _causal_conv(x_ref[...], xbuf, cw_ref, cb_ref, first_tile)
    r = jax.nn.sigmoid(jnp.dot(xc, wa_ref[...], precision=HIGHEST, preferred_element_type=F32) + ba_ref[...])
    i = jax.nn.sigmoid(jnp.dot(xc, wx_ref[...], precision=HIGHEST, preferred_element_type=F32) + bx_ref[...])
    neg_lam = -lam_ref[...]
    softplus = jnp.maximum(neg_lam, 0.0) + jnp.log1p(jnp.exp(-jnp.abs(neg_lam)))
    log_a = -LRU_C * r * softplus
    a = jnp.exp(log_a)
    a_scr[...] = a
    b_scr[...] = jnp.sqrt(-jnp.tanh(log_a) * (a * a + 1.0)) * (i * xc)

    row = lax.broadcasted_iota(jnp.int32, (8, GROUP_WIDTH), 0)

    def body(j, hprev):
        r0 = pl.multiple_of(j * 8, 8)
        a = a_scr[pl.ds(r0, 8), :]
        b = b_scr[pl.ds(r0, 8), :]
        for d in (1, 2, 4):
            a_sh = jnp.where(row >= d, pltpu.roll(a, d, 0), 1.0)
            b_sh = jnp.where(row >= d, pltpu.roll(b, d, 0), 0.0)
            b = a * b_sh + b
            a = a * a_sh
        hh = a * hprev + b
        b_scr[pl.ds(r0, 8), :] = hh
        return jnp.broadcast_to(hh[7:8, :], hh.shape)

    hcar[...] = lax.fori_loop(0, ts // 8, body, hcar[...])
    o_ref[...] = (jax.nn.gelu(g_ref[...], approximate=True) * b_scr[...]).astype(o_ref.dtype)


def _rg_lru(gate, x, conv_w, conv_b, wa, ba, wx, bx, lam, ts=512):
    bsz, s, w = x.shape
    row = pl.BlockSpec((None, ts, w), lambda b, i: (b, i, 0))
    vec = _full((1, w))
    return pl.pallas_call(
        _lru_kernel, grid=(bsz, s // ts),
        in_specs=[row, row, _full((CONV_WIDTH, w)), vec, _full((w, w)), vec, _full((w, w)), vec, vec],
        out_specs=row, out_shape=jax.ShapeDtypeStruct((bsz, s, w), BF16),
        scratch_shapes=[pltpu.VMEM((ts + CONV_PAD, w), F32), pltpu.VMEM((ts, w), F32),
                        pltpu.VMEM((ts, w), F32), pltpu.VMEM((8, w), F32)],
        compiler_params=_params(2),
    )(gate, x, conv_w, conv_b, wa, ba, wx, bx, lam)


def _ssd_kernel(z_ref, xbc_ref, dt_ref, cw_ref, cb_ref, dtb_ref, alog_ref, dsk_ref, ng_ref, o_ref,
                xbuf, st):
    first_tile = pl.program_id(1) == 0
    t = SSM_CHUNK
    gl = GROUP_WIDTH // SSM_GROUPS

    @pl.when(first_tile)
    def _():
        st[...] = jnp.zeros_like(st)

    xc = _causal_conv(xbc_ref[...], xbuf, cw_ref, cb_ref, first_tile)
    xc = xc * jax.nn.sigmoid(xc)
    z = z_ref[...]
    dt_in = dt_ref[...] + dtb_ref[...]
    dtl = jnp.maximum(dt_in, 0.0) + jnp.log1p(jnp.exp(-jnp.abs(dt_in)))
    adt = dtl * (-jnp.exp(alog_ref[...]))
    li = lax.broadcasted_iota(jnp.int32, (t, t), 0)
    si = lax.broadcasted_iota(jnp.int32, (t, t), 1)
    causal = li >= si
    acum = jnp.dot(causal.astype(F32), adt, precision=HIGHEST, preferred_element_type=F32)
    low = lax.broadcasted_iota(jnp.int32, (1, gl), 1) < HEAD_DIM

    for g in range(SSM_GROUPS):
        xg = xc[:, g * gl:(g + 1) * gl]
        bg = xc[:, GROUP_WIDTH + g * SSM_STATE:GROUP_WIDTH + (g + 1) * SSM_STATE]
        cg = xc[:, GROUP_WIDTH + (SSM_GROUPS + g) * SSM_STATE:GROUP_WIDTH + (SSM_GROUPS + g + 1) * SSM_STATE]
        h0, h1 = 2 * g, 2 * g + 1
        xdt = (xg * jnp.where(low, dtl[:, h0 * LANE:(h0 + 1) * LANE], dtl[:, h1 * LANE:(h1 + 1) * LANE])
               ).astype(BF16)
        cgb = cg.astype(BF16)
        cb = lax.dot_general(cgb, bg.astype(BF16), (((1,), (1,)), ((), ())), preferred_element_type=F32)
        state = st[g]
        y_off = jnp.dot(cgb, state.astype(BF16), preferred_element_type=F32)
        y_dg, st_new, ea, cdec = [], [], [], []
        for h in (h0, h1):
            ac = acum[:, h * LANE:(h + 1) * LANE]
            lmat = jnp.exp(jnp.where(causal, ac - ac.T, -jnp.inf))
            y_dg.append(jnp.dot((cb * lmat).astype(BF16), xdt, preferred_element_type=F32))
            a_last = ac[t - 1:t, :]
            bdec = bg * jnp.exp(a_last - ac)
            st_new.append(jnp.dot(bdec.T.astype(BF16), xdt, preferred_element_type=F32))
            ea.append(jnp.exp(ac))
            cdec.append(jnp.exp(a_last))
        y = (jnp.where(low, y_dg[0], y_dg[1]) + y_off * jnp.where(low, ea[0], ea[1])
             + xg * dsk_ref[:, g * gl:(g + 1) * gl])
        st[g] = state * jnp.where(low, cdec[0], cdec[1]) + jnp.where(low, st_new[0], st_new[1])
        zg = z[:, g * gl:(g + 1) * gl]
        y = y * (zg * jax.nn.sigmoid(zg))
        o_ref[:, g * gl:(g + 1) * gl] = _rms(y, ng_ref[:, g * gl:(g + 1) * gl]).astype(o_ref.dtype)


def _mamba2_ssd(z, xbc, dt, conv_w, conv_b, dt_bias, a_log, d_skip, norm_gain):
    bsz, s, w = z.shape
    t = SSM_CHUNK
    row = lambda width: pl.BlockSpec((None, t, width), lambda b, i: (b, i, 0))
    return pl.pallas_call(
        _ssd_kernel, grid=(bsz, s // t),
        in_specs=[row(w), row(SSM_CONV_DIM), row(4 * LANE), _full((CONV_WIDTH, SSM_CONV_DIM)),
                  _full((1, SSM_CONV_DIM)), _full((1, 4 * LANE)), _full((1, 4 * LANE)),
                  _full((1, w)), _full((1, w))],
        out_specs=row(w), out_shape=jax.ShapeDtypeStruct((bsz, s, w), BF16),
        scratch_shapes=[pltpu.VMEM((t + CONV_PAD, SSM_CONV_DIM), F32),
                        pltpu.VMEM((SSM_GROUPS, SSM_STATE, GROUP_WIDTH // SSM_GROUPS), F32)],
        compiler_params=_params(2),
    )(z, xbc, dt, conv_w, conv_b, dt_bias, a_log, d_skip, norm_gain)


def _diff_kernel(tab_ref, q_ref, kt_ref, v_ref, lq1, lk1, lq2, lk2, gain_ref, o_ref,
                 bias_scr, qm_scr, m_scr, l_scr, acc_scr, *, lam_init):
    tq = DIFF_TQ
    qi = pl.program_id(1)
    n_sm = 2 * GROUP_HEADS

    @pl.when((pl.program_id(0) == 0) & (qi == 0))
    def _():
        ri = lax.broadcasted_iota(jnp.int32, (tq, tq), 0)
        ci = lax.broadcasted_iota(jnp.int32, (tq, tq), 1)
        for d in range(DIFF_NEAR):
            dist = d * tq + ri - ci
            for h in range(GROUP_HEADS):
                bias = _bias_from_dist(jnp.maximum(dist, 0), tab_ref, GROUP_HEADS + h)
                bias_scr[d, h] = jnp.where(dist >= 0, bias, -jnp.inf)

    q = q_ref[...]
    lane = lax.broadcasted_iota(jnp.int32, (1, GROUP_WIDTH), 1)
    for idx in range(n_sm):
        sel = (lane // DIFF_QK_DIM) == idx
        qm_scr[idx] = jnp.where(sel, q, jnp.zeros_like(q))
    m_scr[...] = jnp.full(m_scr.shape, -1e30, F32)
    l_scr[...] = jnp.zeros_like(l_scr)
    acc_scr[...] = jnp.zeros_like(acc_scr)

    def block(ki, near):
        kt = kt_ref[ki]
        v = v_ref[pl.ds(pl.multiple_of(ki * tq, tq), tq), :]
        for idx in range(n_sm):
            h = idx // 2
            sc = jnp.dot(qm_scr[idx], kt, preferred_element_type=F32)
            if near:
                sc = sc + bias_scr[qi - ki, h]
            else:
                sc = sc + tab_ref[NUM_BUCKETS - 1, GROUP_HEADS + h]
            m_prev = m_scr[idx]
            m_next = jnp.maximum(m_prev, jnp.max(sc, axis=1, keepdims=True))
            p = jnp.exp(sc - jnp.tile(m_next, (1, tq // LANE)))
            alpha = jnp.exp(m_prev - m_next)
            l_scr[idx] = alpha * l_scr[idx] + jnp.sum(p, axis=1, keepdims=True)
            acc_scr[idx] = (jnp.tile(alpha, (1, GROUP_WIDTH // LANE)) * acc_scr[idx]
                            + jnp.dot(p.astype(BF16), v, preferred_element_type=F32))
            m_scr[idx] = m_next

    n_far = jnp.maximum(qi + 1 - DIFF_NEAR, 0)

    def far_body(ki, carry):
        block(ki, near=False)
        return carry

    def near_body(ki, carry):
        block(ki, near=True)
        return carry

    lax.fori_loop(0, n_far, far_body, 0)
    lax.fori_loop(n_far, qi + 1, near_body, 0)

    lam = (jnp.exp(jnp.sum(lq1[...] * lk1[...], axis=1, keepdims=True))
           - jnp.exp(jnp.sum(lq2[...] * lk2[...], axis=1, keepdims=True)) + lam_init)
    head_of_lane = lane // HEAD_DIM
    rep = GROUP_WIDTH // LANE
    o = jnp.zeros((tq, GROUP_WIDTH), F32)
    for h in range(GROUP_HEADS):
        o1 = acc_scr[2 * h] / jnp.tile(l_scr[2 * h], (1, rep))
        o2 = acc_scr[2 * h + 1] / jnp.tile(l_scr[2 * h + 1], (1, rep))
        o = jnp.where(head_of_lane == h, o1 - lam * o2, o)
    inv = jnp.zeros((tq, GROUP_WIDTH), F32)
    for h in range(GROUP_HEADS):
        hm = head_of_lane == h
        ms = jnp.sum(jnp.where(hm, o * o, 0.0), axis=1, keepdims=True) * (1.0 / HEAD_DIM)
        inv = jnp.where(hm, lax.rsqrt(ms + NORM_EPS), inv)
    o_ref[...] = ((o * inv * gain_ref[...]) * (1.0 - lam_init)).astype(o_ref.dtype)


def _diff_attention(q, kt, v, rel_bias, lq1, lk1, lq2, lk2, gain, lam_init):
    bsz, s, gw = q.shape
    tq = DIFF_TQ
    n_sm = 2 * GROUP_HEADS
    lvec = _full((1, DIFF_QK_DIM))
    qrow = pl.BlockSpec((None, tq, gw), lambda b, i: (b, i, 0))
    return pl.pallas_call(
        functools.partial(_diff_kernel, lam_init=lam_init), grid=(bsz, s // tq),
        in_specs=[pl.BlockSpec(memory_space=pltpu.SMEM), qrow,
                  pl.BlockSpec((None, s // tq, gw, tq), lambda b, i: (b, 0, 0, 0)),
                  pl.BlockSpec((None, s, gw), lambda b, i: (b, 0, 0)),
                  lvec, lvec, lvec, lvec, _full((1, gw))],
        out_specs=qrow, out_shape=jax.ShapeDtypeStruct((bsz, s, gw), BF16),
        scratch_shapes=[pltpu.VMEM((DIFF_NEAR, GROUP_HEADS, tq, tq), F32),
                        pltpu.VMEM((n_sm, tq, gw), BF16),
                        pltpu.VMEM((n_sm, tq, LANE), F32),
                        pltpu.VMEM((n_sm, tq, LANE), F32),
                        pltpu.VMEM((n_sm, tq, gw), F32)],
        compiler_params=_params(2),
    )(rel_bias, q, kt, v, lq1, lk1, lq2, lk2, gain)


def _out_proj_kernel(h_ref, ya, yb, yc, yd, w_ref, g_ref, o_ref):
    gw = GROUP_WIDTH
    acc = jnp.dot(ya[...], w_ref[0:gw, :], preferred_element_type=F32)
    for j, y in enumerate((yb, yc, yd), start=1):
        acc = acc + jnp.dot(y[...], w_ref[j * gw:(j + 1) * gw, :], preferred_element_type=F32)
    o_ref[...] = h_ref[...] + _rms(acc, g_ref[...])


def _out_proj(h, ya, yb, yc, yd, w_out, gain, tm=512):
    bsz, s, d = h.shape
    row = lambda width: pl.BlockSpec((None, tm, width), lambda b, i: (b, i, 0))
    return pl.pallas_call(
        _out_proj_kernel, grid=(bsz, s // tm),
        in_specs=[row(d)] + [row(GROUP_WIDTH)] * 4 + [_full(w_out.shape), _full((1, d))],
        out_specs=row(d), out_shape=jax.ShapeDtypeStruct(h.shape, F32), compiler_params=_params(2),
    )(h, ya, yb, yc, yd, w_out, gain)


def _ffn_kernel(h_ref, g1_ref, wu_ref, wd_ref, g2_ref, o_ref, *, chunk):
    x = h_ref[...]
    u = _rms(x, g1_ref[...]).astype(BF16)
    acc = jnp.zeros(x.shape, F32)
    for c in range(wu_ref.shape[1] // chunk):
        f = jnp.dot(u, wu_ref[:, c * chunk:(c + 1) * chunk], preferred_element_type=F32)
        f = jnp.square(jnp.maximum(f, 0.0)).astype(BF16)
        acc = acc + jnp.dot(f, wd_ref[c * chunk:(c + 1) * chunk, :], preferred_element_type=F32)
    o_ref[...] = x + _rms(acc, g2_ref[...])


def _ffn(h, g1, w_up, w_down, g2, tm=512, chunk=1024):
    bsz, s, d = h.shape
    row = pl.BlockSpec((None, tm, d), lambda b, i: (b, i, 0))
    return pl.pallas_call(
        functools.partial(_ffn_kernel, chunk=chunk), grid=(bsz, s // tm),
        in_specs=[row, _full((1, d)), _full(w_up.shape), _full(w_down.shape), _full((1, d))],
        out_specs=row, out_shape=jax.ShapeDtypeStruct(h.shape, F32), compiler_params=_params(2),
    )(h, g1, w_up, w_down, g2)


def _block_diag(w):
    nb, n, _ = w.shape
    eye = jnp.eye(nb, dtype=w.dtype)
    return (eye[:, None, :, None] * w[:, :, None, :]).reshape(nb * n, nb * n)


def _rep_lanes(v, width):
    return jnp.repeat(v.astype(F32), width)[None, :]


def kernel(x, rel_bias, norm_mix_pre, norm_mix_post, norm_ffn_pre, norm_ffn_post, w_in, w_out,
           lru_conv_w, lru_conv_b, lru_wa, lru_ba, lru_wx, lru_bx, lru_lambda,
           ssm_conv_w, ssm_conv_b, ssm_dt_bias, ssm_a_log, ssm_d, ssm_norm,
           diff_lq1, diff_lk1, diff_lq2, diff_lk2, diff_norm, w_ff_up, w_ff_down):
    depth = w_in.shape[0]
    gw = GROUP_WIDTH
    n_ssm_heads = ssm_dt_bias.shape[1]
    c_dt = 3 * gw + 2 * gw + gw + SSM_CONV_DIM
    vec = lambda p: p.astype(F32)[None, :]
    h = x
    for layer in range(depth):
        w = w_in[layer]
        w_all = jnp.concatenate(
            [w[:, :c_dt], w[:, c_dt + n_ssm_heads:], jnp.repeat(w[:, c_dt:c_dt + n_ssm_heads], LANE, axis=1)],
            axis=1).astype(BF16)
        qa, ka, va, gb, xb, zc, xbc, qd, kdt, vd, dt = _in_proj(h, vec(norm_mix_pre[layer]), w_all)

        ya = _dilated_attention(qa, ka, va, rel_bias)
        yb = _rg_lru(gb, xb, lru_conv_w[layer], vec(lru_conv_b[layer]),
                     _block_diag(lru_wa[layer]).astype(F32), vec(lru_ba[layer]),
                     _block_diag(lru_wx[layer]).astype(F32), vec(lru_bx[layer]), vec(lru_lambda[layer]))
        yc = _mamba2_ssd(zc, xbc, dt, ssm_conv_w[layer], vec(ssm_conv_b[layer]),
                         _rep_lanes(ssm_dt_bias[layer], LANE), _rep_lanes(ssm_a_log[layer], LANE),
                         _rep_lanes(ssm_d[layer], HEAD_DIM), vec(ssm_norm[layer]))
        lam_init = 0.8 - 0.6 * math.exp(-0.3 * layer)
        yd = _diff_attention(qd, kdt, vd, rel_bias, vec(diff_lq1[layer]), vec(diff_lk1[layer]),
                             vec(diff_lq2[layer]), vec(diff_lk2[layer]),
                             jnp.tile(diff_norm[layer].astype(F32), GROUP_HEADS)[None, :], lam_init)

        h = _out_proj(h, ya, yb, yc, yd, w_out[layer].astype(BF16), vec(norm_mix_post[layer]))
        h = _ffn(h, vec(norm_ffn_pre[layer]), w_ff_up[layer].astype(BF16), w_ff_down[layer].astype(BF16),
                 vec(norm_ffn_post[layer]))
    return h
```

```python
import functools
import math

import numpy as np
import jax
import jax.numpy as jnp
from jax import lax
from jax.experimental import pallas as pl
from jax.experimental.pallas import tpu as pltpu

F32 = jnp.float32
BF16 = jnp.bfloat16
HIGHEST = lax.Precision.HIGHEST

NORM_EPS = 1e-6
HEAD_DIM = 64
GROUP_HEADS = 4
GROUP_WIDTH = GROUP_HEADS * HEAD_DIM
NUM_BUCKETS = 32
MAX_DISTANCE = 2048
DILATED_PATTERNS = ((128, 1), (512, 4), (2048, 16))
DIL_BLOCK = 128
DIL_TILE = DIL_BLOCK * max(d for _, d in DILATED_PATTERNS)
LRU_C = 8.0
CONV_WIDTH = 4
SSM_GROUPS = 2
SSM_STATE = 128
SSM_CHUNK = 128
SSM_CONV_DIM = GROUP_WIDTH + 2 * SSM_GROUPS * SSM_STATE
DIFF_QK_DIM = HEAD_DIM // 2
DIFF_TQ = 256
DIFF_ACC_ROWS = HEAD_DIM + 16
LOG2E = math.log2(math.e)
LANE = 128
SUBLANE = 8
CONV_PAD = 8
VMEM_LIMIT = 48 * 1024 * 1024

_C_QA, _C_KA, _C_VA = 0, 256, 512
_C_GB, _C_XB = 768, 1024
_C_ZC, _C_XBC = 1280, 1536
_C_QD, _C_KD, _C_VD = 2304, 2560, 2816
_C_DT = 3072
_C_END = 3584


def _t5_thresholds():
    n = np.arange(1, 4 * MAX_DISTANCE)
    max_exact = NUM_BUCKETS // 2
    large = max_exact + (np.log(n / max_exact) / math.log(MAX_DISTANCE / max_exact)
                         * (NUM_BUCKETS - max_exact)).astype(np.int64)
    bucket = np.where(n < max_exact, n, np.minimum(large, NUM_BUCKETS - 1))
    return tuple(int(n[bucket >= b].min()) for b in range(1, NUM_BUCKETS))


_T5_THR = _t5_thresholds()
DIFF_NEAR = -(-(_T5_THR[-1] + DIFF_TQ - 1) // DIFF_TQ)


def _bias_from_dist(dist, tab_ref, col):
    out = jnp.full(dist.shape, tab_ref[0, col], F32)
    for b in range(1, NUM_BUCKETS):
        out = jnp.where(dist >= _T5_THR[b - 1], tab_ref[b, col], out)
    return out


def _rms(x, g):
    return x * lax.rsqrt(jnp.mean(x * x, axis=-1, keepdims=True) + NORM_EPS) * g


def _params(n_axes):
    return pltpu.CompilerParams(dimension_semantics=("arbitrary",) * n_axes,
                                vmem_limit_bytes=VMEM_LIMIT)


def _full(shape):
    return pl.BlockSpec(shape, lambda *_: (0,) * len(shape))


def _in_proj_kernel(h_ref, g_ref, w_ref, qa, ka, va, gb, xb, zc, xbc, qdt, kd, vdt, dt):
    u = _rms(h_ref[...], g_ref[...]).astype(BF16)

    def seg(lo, hi):
        return jnp.dot(u, w_ref[:, lo:hi], preferred_element_type=F32)

    def store_transposed(ref, val):
        for j in range(ref.shape[0]):
            ref[j] = val[j * DIFF_TQ:(j + 1) * DIFF_TQ, :].T.astype(BF16)

    def store_halves(ref, val):
        for hh in range(ref.shape[0]):
            ref[hh] = val[:, hh * LANE:(hh + 1) * LANE]

    store_halves(qa, seg(_C_QA, _C_KA) * (HEAD_DIM ** -0.5))
    store_halves(ka, seg(_C_KA, _C_VA))
    store_halves(va, seg(_C_VA, _C_GB))
    gb[...] = seg(_C_GB, _C_XB)
    xb[...] = seg(_C_XB, _C_ZC)
    zc[...] = seg(_C_ZC, _C_XBC)
    xbc[...] = seg(_C_XBC, _C_QD)
    store_transposed(qdt, seg(_C_QD, _C_KD) * (DIFF_QK_DIM ** -0.5 * LOG2E))
    kd[...] = seg(_C_KD, _C_VD).astype(BF16)
    store_transposed(vdt, seg(_C_VD, _C_DT))
    dt[...] = seg(_C_DT, _C_END)


def _in_proj(h, gain, w_all, tm=512):
    bsz, s, d = h.shape
    gw = GROUP_WIDTH
    row = lambda width: pl.BlockSpec((None, tm, width), lambda b, i: (b, i, 0))
    shp = lambda width, dt: jax.ShapeDtypeStruct((bsz, s, width), dt)
    tshape = jax.ShapeDtypeStruct((bsz, s // DIFF_TQ, gw, DIFF_TQ), BF16)
    tspec = pl.BlockSpec((None, tm // DIFF_TQ, gw, DIFF_TQ), lambda b, i: (b, i, 0, 0))
    hshape = jax.ShapeDtypeStruct((bsz, gw // LANE, s, LANE), F32)
    hspec = pl.BlockSpec((None, gw // LANE, tm, LANE), lambda b, i: (b, 0, i, 0))
    out_shape = (hshape, hshape, hshape, shp(gw, F32), shp(gw, F32),
                 shp(gw, F32), shp(SSM_CONV_DIM, F32), tshape, shp(gw, BF16), tshape, shp(4 * LANE, F32))
    out_specs = (hspec, hspec, hspec, row(gw), row(gw), row(gw), row(SSM_CONV_DIM),
                 tspec, row(gw), tspec, row(4 * LANE))
    return pl.pallas_call(
        _in_proj_kernel, grid=(bsz, s // tm),
        in_specs=[row(d), _full((1, d)), _full(w_all.shape)],
        out_specs=out_specs, out_shape=out_shape, compiler_params=_params(2), name="in_proj",
    )(h, gain, w_all)


def _dil_kernel(tab_ref, q_ref, k_ref, v_ref, out_ref, bias_scr, kbuf, vbuf, o_scr, lse_scr):
    tile = pl.program_id(1)
    tt = DIL_TILE
    nq, nk = DIL_BLOCK, 2 * DIL_BLOCK
    halves = GROUP_WIDTH // LANE
    heads_per_half = LANE // HEAD_DIM

    @pl.when((pl.program_id(0) == 0) & (tile == 0))
    def _():
        qi = lax.broadcasted_iota(jnp.int32, (nq, nk), 0)
        kj = lax.broadcasted_iota(jnp.int32, (nq, nk), 1)
        rel = qi + DIL_BLOCK - kj
        for p, (window, dil) in enumerate(DILATED_PATTERNS):
            valid = (rel >= 0) & (rel <= window // dil)
            dist = jnp.maximum(rel, 0) * dil
            for h in range(GROUP_HEADS):
                bias_scr[p, h] = jnp.where(valid, _bias_from_dist(dist, tab_ref, h), -jnp.inf)

    @pl.when(tile == 0)
    def _():
        kbuf[:, 0:tt, :] = jnp.zeros((halves, tt, LANE), F32)
        vbuf[:, 0:tt, :] = jnp.zeros((halves, tt, LANE), F32)

    kbuf[:, tt:2 * tt, :] = k_ref[...]
    vbuf[:, tt:2 * tt, :] = v_ref[...]

    head_in_half = lax.broadcasted_iota(jnp.int32, (1, LANE), 1) // HEAD_DIM
    kj = lax.broadcasted_iota(jnp.int32, (1, nk), 1)

    for p, (_, dil) in enumerate(DILATED_PATTERNS):
        nbt = tt // (DIL_BLOCK * dil)

        def body(j, carry, p=p, dil=dil, nbt=nbt):
            r, n = j // nbt, j % nbt
            start = n * (DIL_BLOCK * dil) + r
            kstart = start + tt - DIL_BLOCK * dil
            if dil == 1:
                rows = pl.ds(pl.multiple_of(start, DIL_BLOCK), nq)
                krows = pl.ds(pl.multiple_of(kstart, DIL_BLOCK), nk)
            else:
                rows = pl.ds(start, nq, stride=dil)
                krows = pl.ds(kstart, nk, stride=dil)
            key_ok = kj >= jnp.where(tile * nbt + n > 0, 0, DIL_BLOCK)
            scores = []
            for hh in range(halves):
                q = q_ref[hh, rows, :].astype(BF16)
                k = kbuf[hh, krows, :].astype(BF16)
                for hr in range(heads_per_half):
                    qh = jnp.where(head_in_half == hr, q, jnp.zeros_like(q))
                    scores.append(lax.dot_general(qh, k, (((1,), (1,)), ((), ())), preferred_element_type=F32))
            for hh in range(halves):
                v = vbuf[hh, krows, :].astype(BF16)
                o_acc = jnp.zeros((nq, LANE), F32)
                lse_acc = jnp.zeros((nq, LANE), F32)
                for hr in range(heads_per_half):
                    hm = head_in_half == hr
                    h = hh * heads_per_half + hr
                    sc = jnp.where(key_ok, scores[h] + bias_scr[p, h], -jnp.inf)
                    m = jnp.max(sc, axis=-1, keepdims=True)
                    e = jnp.exp(sc - m)
                    den = jnp.sum(e, axis=-1, keepdims=True)
                    oh = jnp.dot(e.astype(BF16), v, preferred_element_type=F32) / den
                    o_acc = jnp.where(hm, oh, o_acc)
                    lse_acc = jnp.where(hm, m + jnp.log(den), lse_acc)
                o_scr[p, hh, rows, :] = o_acc
                lse_scr[p, hh, rows, :] = lse_acc
            return carry

        lax.fori_loop(0, tt // DIL_BLOCK, body, 0, unroll=4)

    cm = 256

    def combine(c, carry):
        rows = pl.ds(pl.multiple_of(c * cm, cm), cm)
        for hh in range(halves):
            l0, l1, l2 = lse_scr[0, hh, rows, :], lse_scr[1, hh, rows, :], lse_scr[2, hh, rows, :]
            m = jnp.maximum(jnp.maximum(l0, l1), l2)
            w0, w1, w2 = jnp.exp(l0 - m), jnp.exp(l1 - m), jnp.exp(l2 - m)
            num = w0 * o_scr[0, hh, rows, :] + w1 * o_scr[1, hh, rows, :] + w2 * o_scr[2, hh, rows, :]
            out_ref[rows, hh * LANE:(hh + 1) * LANE] = (num / (w0 + w1 + w2)).astype(out_ref.dtype)
        return carry

    lax.fori_loop(0, tt // cm, combine, 0)
    kbuf[:, 0:tt, :] = k_ref[...]
    vbuf[:, 0:tt, :] = v_ref[...]


def _dilated_attention(q, k, v, rel_bias):
    bsz, halves, s, _ = q.shape
    tt = DIL_TILE
    n_pat = len(DILATED_PATTERNS)
    assert s % tt == 0 and n_pat == 3
    hspec = pl.BlockSpec((None, halves, tt, LANE), lambda b, i: (b, 0, i, 0))
    return pl.pallas_call(
        _dil_kernel, grid=(bsz, s // tt),
        in_specs=[pl.BlockSpec(memory_space=pltpu.SMEM), hspec, hspec, hspec],
        out_specs=pl.BlockSpec((None, tt, GROUP_WIDTH), lambda b, i: (b, i, 0)),
        out_shape=jax.ShapeDtypeStruct((bsz, s, GROUP_WIDTH), BF16),
        scratch_shapes=[pltpu.VMEM((n_pat, GROUP_HEADS, DIL_BLOCK, 2 * DIL_BLOCK), F32),
                        pltpu.VMEM((halves, 2 * tt, LANE), F32), pltpu.VMEM((halves, 2 * tt, LANE), F32),
                        pltpu.VMEM((n_pat, halves, tt, LANE), F32),
                        pltpu.VMEM((n_pat, halves, tt, LANE), F32)],
        compiler_params=_params(2), name="dilated_attn",
    )(rel_bias, q, k, v)


def _causal_conv(x, xbuf, cw_ref, cb_ref, first_tile):
    t = x.shape[0]

    @pl.when(first_tile)
    def _():
        xbuf[0:CONV_PAD, :] = jnp.zeros((CONV_PAD, x.shape[1]), F32)

    xbuf[CONV_PAD:CONV_PAD + t, :] = x
    y = cb_ref[...] + cw_ref[CONV_WIDTH - 1:CONV_WIDTH, :] * x
    for kk in range(CONV_WIDTH - 1):
        off = CONV_PAD - (CONV_WIDTH - 1) + kk
        y = y + cw_ref[kk:kk + 1, :] * xbuf[off:off + t, :]
    xbuf[0:CONV_PAD, :] = x[t - CONV_PAD:t, :]
    return y


def _lru_kernel(g_ref, x_ref, cw_ref, cb_ref, wa_ref, ba_ref, wx_ref, bx_ref, lam_ref, o_ref,
                xbuf, a_scr, b_scr, hcar):
    first_tile = pl.program_id(1) == 0
    ts = x_ref.shape[0]

    @pl.when(first_tile)
    def _():
        hcar[...] = jnp.zeros_like(hcar)

    xc = _causal_conv(x_ref[...], xbuf, cw_ref, cb_ref, first_tile)
    r = jax.nn.sigmoid(jnp.dot(xc, wa_ref[...], precision=HIGHEST, preferred_element_type=F32) + ba_ref[...])
    i = jax.nn.sigmoid(jnp.dot(xc, wx_ref[...], precision=HIGHEST, preferred_element_type=F32) + bx_ref[...])
    neg_lam = -lam_ref[...]
    softplus = jnp.maximum(neg_lam, 0.0) + jnp.log1p(jnp.exp(-jnp.abs(neg_lam)))
    log_a = -LRU_C * r * softplus
    a = jnp.exp(log_a)
    a_scr[...] = a
    b_scr[...] = jnp.sqrt(-jnp.tanh(log_a) * (a * a + 1.0)) * (i * xc)

    row = lax.broadcasted_iota(jnp.int32, (SUBLANE, GROUP_WIDTH), 0)

    def body(j, hprev):
        r0 = pl.multiple_of(j * SUBLANE, SUBLANE)
        a = a_scr[pl.ds(r0, SUBLANE), :]
        b = b_scr[pl.ds(r0, SUBLANE), :]
        for d in (1, 2, 4):
            a_sh = jnp.where(row >= d, pltpu.roll(a, d, 0), 1.0)
            b_sh = jnp.where(row >= d, pltpu.roll(b, d, 0), 0.0)
            b = a * b_sh + b
            a = a * a_sh
        hh = a * hprev + b
        b_scr[pl.ds(r0, SUBLANE), :] = hh
        return jnp.broadcast_to(hh[SUBLANE - 1:SUBLANE, :], hh.shape)

    hcar[...] = lax.fori_loop(0, ts // SUBLANE, body, hcar[...])
    o_ref[...] = (jax.nn.gelu(g_ref[...], approximate=True) * b_scr[...]).astype(o_ref.dtype)


def _rg_lru(gate, x, conv_w, conv_b, wa, ba, wx, bx, lam, ts=512):
    bsz, s, w = x.shape
    row = pl.BlockSpec((None, ts, w), lambda b, i: (b, i, 0))
    vec = _full((1, w))
    return pl.pallas_call(
        _lru_kernel, grid=(bsz, s // ts),
        in_specs=[row, row, _full((CONV_WIDTH, w)), vec, _full((w, w)), vec, _full((w, w)), vec, vec],
        out_specs=row, out_shape=jax.ShapeDtypeStruct((bsz, s, w), BF16),
        scratch_shapes=[pltpu.VMEM((ts + CONV_PAD, w), F32), pltpu.VMEM((ts, w), F32),
                        pltpu.VMEM((ts, w), F32), pltpu.VMEM((SUBLANE, w), F32)],
        compiler_params=_params(2), name="rg_lru",
    )(gate, x, conv_w, conv_b, wa, ba, wx, bx, lam)


def _ssd_kernel(z_ref, xbc_ref, dt_ref, cw_ref, cb_ref, dtb_ref, alog_ref, dsk_ref, ng_ref, o_ref,
                xbuf, st):
    first_tile = pl.program_id(1) == 0
    t = SSM_CHUNK
    gl = GROUP_WIDTH // SSM_GROUPS

    @pl.when(first_tile)
    def _():
        st[...] = jnp.zeros_like(st)

    xc = _causal_conv(xbc_ref[...], xbuf, cw_ref, cb_ref, first_tile)
    xc = xc * jax.nn.sigmoid(xc)
    z = z_ref[...]
    dt_in = dt_ref[...] + dtb_ref[...]
    dtl = jnp.maximum(dt_in, 0.0) + jnp.log1p(jnp.exp(-jnp.abs(dt_in)))
    adt = dtl * (-jnp.exp(alog_ref[...]))
    li = lax.broadcasted_iota(jnp.int32, (t, t), 0)
    si = lax.broadcasted_iota(jnp.int32, (t, t), 1)
    causal = li >= si
    acum = jnp.dot(causal.astype(F32), adt, precision=HIGHEST, preferred_element_type=F32)
    low = lax.broadcasted_iota(jnp.int32, (1, gl), 1) < HEAD_DIM

    for g in range(SSM_GROUPS):
        xg = xc[:, g * gl:(g + 1) * gl]
        bg = xc[:, GROUP_WIDTH + g * SSM_STATE:GROUP_WIDTH + (g + 1) * SSM_STATE]
        cg = xc[:, GROUP_WIDTH + (SSM_GROUPS + g) * SSM_STATE:GROUP_WIDTH + (SSM_GROUPS + g + 1) * SSM_STATE]
        h0, h1 = 2 * g, 2 * g + 1
        xdt = (xg * jnp.where(low, dtl[:, h0 * LANE:(h0 + 1) * LANE], dtl[:, h1 * LANE:(h1 + 1) * LANE])
               ).astype(BF16)
        cgb = cg.astype(BF16)
        cb = lax.dot_general(cgb, bg.astype(BF16), (((1,), (1,)), ((), ())), preferred_element_type=F32)
        state = st[g]
        y_off = jnp.dot(cgb, state.astype(BF16), preferred_element_type=F32)
        y_dg, st_new, ea, cdec = [], [], [], []
        for h in (h0, h1):
            ac = acum[:, h * LANE:(h + 1) * LANE]
            lmat = jnp.exp(jnp.where(causal, ac - ac.T, -jnp.inf))
            y_dg.append(jnp.dot((cb * lmat).astype(BF16), xdt, preferred_element_type=F32))
            a_last = ac[t - 1:t, :]
            bdec = bg * jnp.exp(a_last - ac)
            st_new.append(jnp.dot(bdec.T.astype(BF16), xdt, preferred_element_type=F32))
            ea.append(jnp.exp(ac))
            cdec.append(jnp.exp(a_last))
        y = (jnp.where(low, y_dg[0], y_dg[1]) + y_off * jnp.where(low, ea[0], ea[1])
             + xg * dsk_ref[:, g * gl:(g + 1) * gl])
        st[g] = state * jnp.where(low, cdec[0], cdec[1]) + jnp.where(low, st_new[0], st_new[1])
        zg = z[:, g * gl:(g + 1) * gl]
        y = y * (zg * jax.nn.sigmoid(zg))
        o_ref[:, g * gl:(g + 1) * gl] = _rms(y, ng_ref[:, g * gl:(g + 1) * gl]).astype(o_ref.dtype)


def _mamba2_ssd(z, xbc, dt, conv_w, conv_b, dt_bias, a_log, d_skip, norm_gain):
    bsz, s, w = z.shape
    t = SSM_CHUNK
    row = lambda width: pl.BlockSpec((None, t, width), lambda b, i: (b, i, 0))
    return pl.pallas_call(
        _ssd_kernel, grid=(bsz, s // t),
        in_specs=[row(w), row(SSM_CONV_DIM), row(4 * LANE), _full((CONV_WIDTH, SSM_CONV_DIM)),
                  _full((1, SSM_CONV_DIM)), _full((1, 4 * LANE)), _full((1, 4 * LANE)),
                  _full((1, w)), _full((1, w))],
        out_specs=row(w), out_shape=jax.ShapeDtypeStruct((bsz, s, w), BF16),
        scratch_shapes=[pltpu.VMEM((t + CONV_PAD, SSM_CONV_DIM), F32),
                        pltpu.VMEM((SSM_GROUPS, SSM_STATE, GROUP_WIDTH // SSM_GROUPS), F32)],
        compiler_params=_params(2), name="ssd",
    )(z, xbc, dt, conv_w, conv_b, dt_bias, a_log, d_skip, norm_gain)


def _diff_kernel(tab_ref, qt_ref, k_ref, vt_ref, lq1, lk1, lq2, lk2, gain_ref, o_ref,
                 bias_scr, qm_scr, s_a, s_b, s_c, mc_a, mc_b, mc_c, m_scr, acc_scr, ot_scr, *, lam_init):
    tq = DIFF_TQ
    qi = pl.program_id(1)
    n_sm = 2 * GROUP_HEADS
    grp = tq // SUBLANE
    acc_grp = DIFF_ACC_ROWS // SUBLANE
    s_bufs, mc_bufs = (s_a, s_b, s_c), (mc_a, mc_b, mc_c)
    nbuf = len(s_bufs)

    @pl.when((pl.program_id(0) == 0) & (qi == 0))
    def _():
        ki_ = lax.broadcasted_iota(jnp.int32, (tq, tq), 0)
        qi_ = lax.broadcasted_iota(jnp.int32, (tq, tq), 1)
        for h in range(GROUP_HEADS):
            for d in range(DIFF_NEAR):
                dist = d * tq + qi_ - ki_
                bias = _bias_from_dist(jnp.maximum(dist, 0), tab_ref, GROUP_HEADS + h) * LOG2E
                bias_scr[d, h] = jnp.where(dist >= 0, bias, -jnp.inf)
            bias_scr[DIFF_NEAR, h] = jnp.full((tq, tq), tab_ref[NUM_BUCKETS - 1, GROUP_HEADS + h], F32) * LOG2E
            bias_scr[DIFF_NEAR + 1, h] = jnp.full((tq, tq), -jnp.inf, F32)

    qt = qt_ref[...]
    feat = lax.broadcasted_iota(jnp.int32, (GROUP_WIDTH, 1), 0)
    for idx in range(n_sm):
        qm_scr[idx] = jnp.where((feat // DIFF_QK_DIM) == idx, qt, jnp.zeros_like(qt))
    m_scr[...] = jnp.full(m_scr.shape, -1e30, F32)
    acc_scr[...] = jnp.zeros_like(acc_scr)

    c_far = [jnp.full((SUBLANE, tq), tab_ref[NUM_BUCKETS - 1, GROUP_HEADS + h], F32) * LOG2E
             for h in range(GROUP_HEADS)]
    ones_rows = jnp.ones((DIFF_ACC_ROWS - HEAD_DIM, tq), BF16)
    n_far = jnp.maximum(qi + 1 - DIFF_NEAR, 0)
    j_far = lax.div(jnp.maximum(n_far - 1, 0), nbuf)
    n_raw = nbuf * j_far

    def rows_max(x3):
        part = jnp.max(x3, axis=0)
        return jnp.broadcast_to(jnp.max(part, axis=0, keepdims=True), part.shape)

    def rows_sum(x3):
        part = jnp.sum(x3, axis=0)
        return jnp.broadcast_to(jnp.sum(part, axis=0, keepdims=True), part.shape)

    def scores(ki, s_buf):
        kb = k_ref[pl.ds(pl.multiple_of(ki * tq, tq), tq), :]
        for idx in range(n_sm):
            s_buf[idx] = jnp.dot(kb, qm_scr[idx], preferred_element_type=F32)

    def max_any(ki, s_buf, mc):
        d = qi - ki
        tile = jnp.where(d < 0, DIFF_NEAR + 1, jnp.minimum(d, DIFF_NEAR))
        for idx in range(n_sm):
            t = s_buf[idx] + bias_scr[tile, idx // 2]
            s_buf[idx] = t
            mc[idx] = rows_max(t.reshape(grp, SUBLANE, tq))

    def max_far(ki, s_buf, mc):
        for idx in range(n_sm):
            mc[idx] = rows_max(s_buf[idx].reshape(grp, SUBLANE, tq)) + c_far[idx // 2]

    def exp_pass(ki, s_buf, mc):
        vt = vt_ref[jnp.minimum(ki, qi)]
        raw = (ki >= 1) & (ki <= n_raw)
        for idx in range(n_sm):
            h = idx // 2
            m_prev = m_scr[idx]
            m_next = jnp.maximum(m_prev, mc[idx])
            shift = m_next - jnp.where(raw, c_far[h], 0.0)
            p3 = jnp.exp2(s_buf[idx].reshape(grp, SUBLANE, tq) - shift[None])
            alpha = jnp.exp2(m_prev - m_next)
            lhs = jnp.concatenate([vt[h * HEAD_DIM:(h + 1) * HEAD_DIM, :], ones_rows], axis=0)
            pv = jnp.dot(lhs, p3.reshape(tq, tq).astype(BF16), preferred_element_type=F32)
            acc = acc_scr[idx].reshape(acc_grp, SUBLANE, tq) * alpha[None]
            acc_scr[idx] = acc.reshape(DIFF_ACC_ROWS, tq) + pv
            m_scr[idx] = m_next

    def trip(j, carry, max_pass):
        for u in range(nbuf):
            b = nbuf * j + u
            scores(jnp.minimum(b + 2, qi), s_bufs[(u + 2) % nbuf])
            max_pass(b + 1, s_bufs[(u + 1) % nbuf], mc_bufs[(u + 1) % nbuf])
            exp_pass(b, s_bufs[u], mc_bufs[u])
        return carry

    scores(0, s_bufs[0])
    scores(jnp.minimum(1, qi), s_bufs[1])
    max_any(0, s_bufs[0], mc_bufs[0])
    lax.fori_loop(0, j_far, functools.partial(trip, max_pass=max_far), 0)
    lax.fori_loop(j_far, lax.div(qi + nbuf, nbuf), functools.partial(trip, max_pass=max_any), 0)

    lam = (jnp.exp(jnp.sum(lq1[...] * lk1[...], axis=1, keepdims=True))
           - jnp.exp(jnp.sum(lq2[...] * lk2[...], axis=1, keepdims=True)) + lam_init)
    for h in range(GROUP_HEADS):
        shape3 = (HEAD_DIM // SUBLANE, SUBLANE, tq)
        a1, a2 = acc_scr[2 * h], acc_scr[2 * h + 1]
        o1 = a1[0:HEAD_DIM].reshape(shape3) / a1[HEAD_DIM:HEAD_DIM + SUBLANE][None]
        o2 = a2[0:HEAD_DIM].reshape(shape3) / a2[HEAD_DIM:HEAD_DIM + SUBLANE][None]
        oh = o1 - lam * o2
        ms = rows_sum(oh * oh) * (1.0 / HEAD_DIM)
        oh = oh * lax.rsqrt(ms + NORM_EPS)[None]
        ot_scr[h * HEAD_DIM:(h + 1) * HEAD_DIM, :] = oh.reshape(HEAD_DIM, tq)
    o_ref[...] = ((ot_scr[...].T * gain_ref[...]) * (1.0 - lam_init)).astype(o_ref.dtype)


def _diff_attention(qt, k, vt, rel_bias, lq1, lk1, lq2, lk2, gain, lam_init):
    bsz, s, gw = k.shape
    tq = DIFF_TQ
    n_sm = 2 * GROUP_HEADS
    lvec = _full((1, DIFF_QK_DIM))
    stat = pltpu.VMEM((n_sm, SUBLANE, tq), F32)
    return pl.pallas_call(
        functools.partial(_diff_kernel, lam_init=lam_init), grid=(bsz, s // tq),
        in_specs=[pl.BlockSpec(memory_space=pltpu.SMEM),
                  pl.BlockSpec((None, None, gw, tq), lambda b, i: (b, i, 0, 0)),
                  pl.BlockSpec((None, s, gw), lambda b, i: (b, 0, 0)),
                  pl.BlockSpec((None, s // tq, gw, tq), lambda b, i: (b, 0, 0, 0)),
                  lvec, lvec, lvec, lvec, _full((1, gw))],
        out_specs=pl.BlockSpec((None, tq, gw), lambda b, i: (b, i, 0)),
        out_shape=jax.ShapeDtypeStruct((bsz, s, gw), BF16),
        scratch_shapes=[pltpu.VMEM((DIFF_NEAR + 2, GROUP_HEADS, tq, tq), F32),
                        pltpu.VMEM((n_sm, gw, tq), BF16),
                        pltpu.VMEM((n_sm, tq, tq), F32), pltpu.VMEM((n_sm, tq, tq), F32),
                        pltpu.VMEM((n_sm, tq, tq), F32), stat, stat, stat, stat,
                        pltpu.VMEM((n_sm, DIFF_ACC_ROWS, tq), F32),
                        pltpu.VMEM((gw, tq), F32)],
        compiler_params=_params(2), name="diff_attn",
    )(rel_bias, qt, k, vt, lq1, lk1, lq2, lk2, gain)


def _out_proj_kernel(h_ref, ya, yb, yc, yd, w_ref, g_ref, o_ref):
    gw = GROUP_WIDTH
    acc = jnp.dot(ya[...], w_ref[0:gw, :], preferred_element_type=F32)
    for j, y in enumerate((yb, yc, yd), start=1):
        acc = acc + jnp.dot(y[...], w_ref[j * gw:(j + 1) * gw, :], preferred_element_type=F32)
    o_ref[...] = h_ref[...] + _rms(acc, g_ref[...])


def _out_proj(h, ya, yb, yc, yd, w_out, gain, tm=512):
    bsz, s, d = h.shape
    row = lambda width: pl.BlockSpec((None, tm, width), lambda b, i: (b, i, 0))
    return pl.pallas_call(
        _out_proj_kernel, grid=(bsz, s // tm),
        in_specs=[row(d)] + [row(GROUP_WIDTH)] * 4 + [_full(w_out.shape), _full((1, d))],
        out_specs=row(d), out_shape=jax.ShapeDtypeStruct(h.shape, F32), compiler_params=_params(2),
        name="out_proj",
    )(h, ya, yb, yc, yd, w_out, gain)


def _ffn_kernel(h_ref, g1_ref, wu_ref, wd_ref, g2_ref, o_ref, *, chunk):
    x = h_ref[...]
    u = _rms(x, g1_ref[...]).astype(BF16)
    acc = jnp.zeros(x.shape, F32)
    for c in range(wu_ref.shape[1] // chunk):
        f = jnp.dot(u, wu_ref[:, c * chunk:(c + 1) * chunk], preferred_element_type=F32)
        f = jnp.square(jnp.maximum(f, 0.0)).astype(BF16)
        acc = acc + jnp.dot(f, wd_ref[c * chunk:(c + 1) * chunk, :], preferred_element_type=F32)
    o_ref[...] = x + _rms(acc, g2_ref[...])


def _ffn(h, g1, w_up, w_down, g2, tm=512, chunk=1024):
    bsz, s, d = h.shape
    row = pl.BlockSpec((None, tm, d), lambda b, i: (b, i, 0))
    return pl.pallas_call(
        functools.partial(_ffn_kernel, chunk=chunk), grid=(bsz, s // tm),
        in_specs=[row, _full((1, d)), _full(w_up.shape), _full(w_down.shape), _full((1, d))],
        out_specs=row, out_shape=jax.ShapeDtypeStruct(h.shape, F32), compiler_params=_params(2),
        name="ffn",
    )(h, g1, w_up, w_down, g2)


def _block_diag(w):
    nb, n, _ = w.shape
    eye = jnp.eye(nb, dtype=w.dtype)
    return (eye[:, None, :, None] * w[:, :, None, :]).reshape(nb * n, nb * n)


def _rep_lanes(v, width):
    return jnp.repeat(v.astype(F32), width)[None, :]


def kernel(x, rel_bias, norm_mix_pre, norm_mix_post, norm_ffn_pre, norm_ffn_post, w_in, w_out,
           lru_conv_w, lru_conv_b, lru_wa, lru_ba, lru_wx, lru_bx, lru_lambda,
           ssm_conv_w, ssm_conv_b, ssm_dt_bias, ssm_a_log, ssm_d, ssm_norm,
           diff_lq1, diff_lk1, diff_lq2, diff_lk2, diff_norm, w_ff_up, w_ff_down):
    depth = w_in.shape[0]
    gw = GROUP_WIDTH
    n_ssm_heads = ssm_dt_bias.shape[1]
    c_dt = 3 * gw + 2 * gw + gw + SSM_CONV_DIM
    vec = lambda p: p.astype(F32)[None, :]
    h = x
    for layer in range(depth):
        w = w_in[layer]
        w_all = jnp.concatenate(
            [w[:, :c_dt], w[:, c_dt + n_ssm_heads:], jnp.repeat(w[:, c_dt:c_dt + n_ssm_heads], LANE, axis=1)],
            axis=1).astype(BF16)
        qa, ka, va, gb, xb, zc, xbc, qdt, kd, vdt, dt = _in_proj(h, vec(norm_mix_pre[layer]), w_all)

        ya = _dilated_attention(qa, ka, va, rel_bias)
        yb = _rg_lru(gb, xb, lru_conv_w[layer], vec(lru_conv_b[layer]),
                     _block_diag(lru_wa[layer]).astype(F32), vec(lru_ba[layer]),
                     _block_diag(lru_wx[layer]).astype(F32), vec(lru_bx[layer]), vec(lru_lambda[layer]))
        yc = _mamba2_ssd(zc, xbc, dt, ssm_conv_w[layer], vec(ssm_conv_b[layer]),
                         _rep_lanes(ssm_dt_bias[layer], LANE), _rep_lanes(ssm_a_log[layer], LANE),
                         _rep_lanes(ssm_d[layer], HEAD_DIM), vec(ssm_norm[layer]))
        lam_init = 0.8 - 0.6 * math.exp(-0.3 * layer)
        yd = _diff_attention(qdt, kd, vdt, rel_bias, vec(diff_lq1[layer]), vec(diff_lk1[layer]),
                             vec(diff_lq2[layer]), vec(diff_lk2[layer]),
                             jnp.tile(diff_norm[layer].astype(F32), GROUP_HEADS)[None, :], lam_init)

        h = _out_proj(h, ya, yb, yc, yd, w_out[layer].astype(BF16), vec(norm_mix_post[layer]))
        h = _ffn(h, vec(norm_ffn_pre[layer]), w_ff_up[layer].astype(BF16), w_ff_down[layer].astype(BF16),
                 vec(norm_ffn_post[layer]))
    return h
```

```python
import functools
import math

import numpy as np
import jax
import jax.numpy as jnp
from jax import lax
from jax.experimental import pallas as pl
from jax.experimental.pallas import tpu as pltpu

F32 = jnp.float32
BF16 = jnp.bfloat16
HIGHEST = lax.Precision.HIGHEST

NORM_EPS = 1e-6
HEAD_DIM = 64
GROUP_HEADS = 4
GROUP_WIDTH = GROUP_HEADS * HEAD_DIM
NUM_BUCKETS = 32
MAX_DISTANCE = 2048
DILATED_PATTERNS = ((128, 1), (512, 4), (2048, 16))
DIL_BLOCK = 128
DIL_TILE = DIL_BLOCK * max(d for _, d in DILATED_PATTERNS)
LRU_C = 8.0
CONV_WIDTH = 4
SSM_GROUPS = 2
SSM_STATE = 128
SSM_CHUNK = 128
SSM_CONV_DIM = GROUP_WIDTH + 2 * SSM_GROUPS * SSM_STATE
DIFF_QK_DIM = HEAD_DIM // 2
DIFF_TQ = 256
DIFF_ACC_ROWS = HEAD_DIM + 16
LOG2E = math.log2(math.e)
LANE = 128
SUBLANE = 8
CONV_PAD = 8
VMEM_LIMIT = 48 * 1024 * 1024

_C_QA, _C_KA, _C_VA = 0, 256, 512
_C_GB, _C_XB = 768, 1024
_C_ZC, _C_XBC = 1280, 1536
_C_QD, _C_KD, _C_VD = 2304, 2560, 2816
_C_DT = 3072
_C_END = 3200
_W_IN_DT = 3 * GROUP_WIDTH + 2 * GROUP_WIDTH + GROUP_WIDTH + SSM_CONV_DIM
SSM_HEADS = GROUP_HEADS


def _t5_thresholds():
    n = np.arange(1, 4 * MAX_DISTANCE)
    max_exact = NUM_BUCKETS // 2
    large = max_exact + (np.log(n / max_exact) / math.log(MAX_DISTANCE / max_exact)
                         * (NUM_BUCKETS - max_exact)).astype(np.int64)
    bucket = np.where(n < max_exact, n, np.minimum(large, NUM_BUCKETS - 1))
    return tuple(int(n[bucket >= b].min()) for b in range(1, NUM_BUCKETS))


_T5_THR = _t5_thresholds()
DIFF_NEAR = -(-(_T5_THR[-1] + DIFF_TQ - 1) // DIFF_TQ)


def _bias_from_dist(dist, tab_ref, col):
    out = jnp.full(dist.shape, tab_ref[0, col], F32)
    for b in range(1, NUM_BUCKETS):
        out = jnp.where(dist >= _T5_THR[b - 1], tab_ref[b, col], out)
    return out


def _rms(x, g):
    return x * lax.rsqrt(jnp.mean(x * x, axis=-1, keepdims=True) + NORM_EPS) * g


def _params(n_axes):
    return pltpu.CompilerParams(dimension_semantics=("arbitrary",) * n_axes,
                                vmem_limit_bytes=VMEM_LIMIT)


def _full(shape):
    return pl.BlockSpec(shape, lambda *_: (0,) * len(shape))


def _resident(shape):
    return pl.BlockSpec(shape, lambda *_: (0,) * len(shape), pipeline_mode=pl.Buffered(1))


def _pack_w_in_kernel(w_ref, o_ref):
    d_cols = _C_DT - _W_IN_DT
    o_ref[:, 0:_W_IN_DT] = w_ref[:, 0:_W_IN_DT].astype(BF16)
    o_ref[:, _W_IN_DT:_C_DT] = w_ref[:, _W_IN_DT + SSM_HEADS:_W_IN_DT + SSM_HEADS + d_cols].astype(BF16)
    lane = lax.broadcasted_iota(jnp.int32, (1, LANE), 1)
    dt_tile = w_ref[:, _W_IN_DT:_W_IN_DT + LANE]
    o_ref[:, _C_DT:_C_END] = jnp.where(lane < SSM_HEADS, dt_tile, 0.0).astype(BF16)


def _pack_w_in(w_in, layer, tr=256):
    _, d, p_in = w_in.shape
    assert p_in == _C_DT + SSM_HEADS
    return pl.pallas_call(
        _pack_w_in_kernel, grid=(d // tr,),
        in_specs=[pl.BlockSpec((None, tr, p_in), lambda i: (layer, i, 0))],
        out_specs=pl.BlockSpec((tr, _C_END), lambda i: (i, 0)),
        out_shape=jax.ShapeDtypeStruct((d, _C_END), BF16), compiler_params=_params(1), name="pack_w_in",
    )(w_in)


def _in_proj_kernel(h_ref, g_ref, w_ref, qa, ka, va, gb, xb, zc, xbc, qdt, kd, vdt, dt):
    u = _rms(h_ref[...], g_ref[...]).astype(BF16)

    def seg(lo, hi):
        return jnp.dot(u, w_ref[:, lo:hi], preferred_element_type=F32)

    def store_transposed(ref, val):
        for j in range(ref.shape[0]):
            ref[j] = val[j * DIFF_TQ:(j + 1) * DIFF_TQ, :].T.astype(BF16)

    def store_halves(ref, val):
        for hh in range(ref.shape[0]):
            ref[hh] = val[:, hh * LANE:(hh + 1) * LANE]

    store_halves(qa, seg(_C_QA, _C_KA) * (HEAD_DIM ** -0.5))
    store_halves(ka, seg(_C_KA, _C_VA))
    store_halves(va, seg(_C_VA, _C_GB))
    gb[...] = seg(_C_GB, _C_XB)
    xb[...] = seg(_C_XB, _C_ZC)
    zc[...] = seg(_C_ZC, _C_XBC)
    xbc[...] = seg(_C_XBC, _C_QD)
    store_transposed(qdt, seg(_C_QD, _C_KD) * (DIFF_QK_DIM ** -0.5 * LOG2E))
    kd[...] = seg(_C_KD, _C_VD).astype(BF16)
    store_transposed(vdt, seg(_C_VD, _C_DT))
    dt[...] = seg(_C_DT, _C_END)


def _in_proj(h, gain, w_all, tm=512):
    bsz, s, d = h.shape
    gw = GROUP_WIDTH
    row = lambda width: pl.BlockSpec((None, tm, width), lambda b, i: (b, i, 0))
    shp = lambda width, dt: jax.ShapeDtypeStruct((bsz, s, width), dt)
    tshape = jax.ShapeDtypeStruct((bsz, s // DIFF_TQ, gw, DIFF_TQ), BF16)
    tspec = pl.BlockSpec((None, tm // DIFF_TQ, gw, DIFF_TQ), lambda b, i: (b, i, 0, 0))
    hshape = jax.ShapeDtypeStruct((bsz, gw // LANE, s, LANE), F32)
    hspec = pl.BlockSpec((None, gw // LANE, tm, LANE), lambda b, i: (b, 0, i, 0))
    out_shape = (hshape, hshape, hshape, shp(gw, F32), shp(gw, F32),
                 shp(gw, F32), shp(SSM_CONV_DIM, F32), tshape, shp(gw, BF16), tshape, shp(LANE, F32))
    out_specs = (hspec, hspec, hspec, row(gw), row(gw), row(gw), row(SSM_CONV_DIM),
                 tspec, row(gw), tspec, row(LANE))
    return pl.pallas_call(
        _in_proj_kernel, grid=(bsz, s // tm),
        in_specs=[row(d), _full((1, d)), _resident(w_all.shape)],
        out_specs=out_specs, out_shape=out_shape, compiler_params=_params(2), name="in_proj",
    )(h, gain, w_all)


def _dil_kernel(tab_ref, q_ref, k_ref, v_ref, out_ref, bias_scr, kbuf, vbuf, o_scr, lse_scr):
    tile = pl.program_id(1)
    tt = DIL_TILE
    nq, nk = DIL_BLOCK, 2 * DIL_BLOCK
    halves = GROUP_WIDTH // LANE
    heads_per_half = LANE // HEAD_DIM

    @pl.when((pl.program_id(0) == 0) & (tile == 0))
    def _():
        qi = lax.broadcasted_iota(jnp.int32, (nq, nk), 0)
        kj = lax.broadcasted_iota(jnp.int32, (nq, nk), 1)
        rel = qi + DIL_BLOCK - kj
        for p, (window, dil) in enumerate(DILATED_PATTERNS):
            valid = (rel >= 0) & (rel <= window // dil)
            dist = jnp.maximum(rel, 0) * dil
            for h in range(GROUP_HEADS):
                bias_scr[p, h] = jnp.where(valid, _bias_from_dist(dist, tab_ref, h), -jnp.inf)

    @pl.when(tile == 0)
    def _():
        kbuf[:, 0:tt, :] = jnp.zeros((halves, tt, LANE), F32)
        vbuf[:, 0:tt, :] = jnp.zeros((halves, tt, LANE), F32)

    kbuf[:, tt:2 * tt, :] = k_ref[...]
    vbuf[:, tt:2 * tt, :] = v_ref[...]

    head_in_half = lax.broadcasted_iota(jnp.int32, (1, LANE), 1) // HEAD_DIM
    kj = lax.broadcasted_iota(jnp.int32, (1, nk), 1)

    for p, (_, dil) in enumerate(DILATED_PATTERNS):
        nbt = tt // (DIL_BLOCK * dil)

        def body(j, carry, p=p, dil=dil, nbt=nbt):
            r, n = j // nbt, j % nbt
            start = n * (DIL_BLOCK * dil) + r
            kstart = start + tt - DIL_BLOCK * dil
            if dil == 1:
                rows = pl.ds(pl.multiple_of(start, DIL_BLOCK), nq)
                krows = pl.ds(pl.multiple_of(kstart, DIL_BLOCK), nk)
            else:
                rows = pl.ds(start, nq, stride=dil)
                krows = pl.ds(kstart, nk, stride=dil)
            key_ok = kj >= jnp.where(tile * nbt + n > 0, 0, DIL_BLOCK)
            scores = []
            for hh in range(halves):
                q = q_ref[hh, rows, :].astype(BF16)
                k = kbuf[hh, krows, :].astype(BF16)
                for hr in range(heads_per_half):
                    qh = jnp.where(head_in_half == hr, q, jnp.zeros_like(q))
                    scores.append(lax.dot_general(qh, k, (((1,), (1,)), ((), ())), preferred_element_type=F32))
            for hh in range(halves):
                v = vbuf[hh, krows, :].astype(BF16)
                o_acc = jnp.zeros((nq, LANE), F32)
                lse_acc = jnp.zeros((nq, LANE), F32)
                for hr in range(heads_per_half):
                    hm = head_in_half == hr
                    h = hh * heads_per_half + hr
                    sc = jnp.where(key_ok, scores[h] + bias_scr[p, h], -jnp.inf)
                    m = jnp.max(sc, axis=-1, keepdims=True)
                    e = jnp.exp(sc - m)
                    den = jnp.sum(e, axis=-1, keepdims=True)
                    oh = jnp.dot(e.astype(BF16), v, preferred_element_type=F32) / den
                    o_acc = jnp.where(hm, oh, o_acc)
                    lse_acc = jnp.where(hm, m + jnp.log(den), lse_acc)
                o_scr[p, hh, rows, :] = o_acc
                lse_scr[p, hh, rows, :] = lse_acc
            return carry

        lax.fori_loop(0, tt // DIL_BLOCK, body, 0, unroll=4)

    cm = 256

    def combine(c, carry):
        rows = pl.ds(pl.multiple_of(c * cm, cm), cm)
        for hh in range(halves):
            l0, l1, l2 = lse_scr[0, hh, rows, :], lse_scr[1, hh, rows, :], lse_scr[2, hh, rows, :]
            m = jnp.maximum(jnp.maximum(l0, l1), l2)
            w0, w1, w2 = jnp.exp(l0 - m), jnp.exp(l1 - m), jnp.exp(l2 - m)
            num = w0 * o_scr[0, hh, rows, :] + w1 * o_scr[1, hh, rows, :] + w2 * o_scr[2, hh, rows, :]
            out_ref[rows, hh * LANE:(hh + 1) * LANE] = (num / (w0 + w1 + w2)).astype(out_ref.dtype)
        return carry

    lax.fori_loop(0, tt // cm, combine, 0)
    kbuf[:, 0:tt, :] = k_ref[...]
    vbuf[:, 0:tt, :] = v_ref[...]


def _dilated_attention(q, k, v, rel_bias):
    bsz, halves, s, _ = q.shape
    tt = DIL_TILE
    n_pat = len(DILATED_PATTERNS)
    assert s % tt == 0 and n_pat == 3
    hspec = pl.BlockSpec((None, halves, tt, LANE), lambda b, i: (b, 0, i, 0))
    return pl.pallas_call(
        _dil_kernel, grid=(bsz, s // tt),
        in_specs=[pl.BlockSpec(memory_space=pltpu.SMEM), hspec, hspec, hspec],
        out_specs=pl.BlockSpec((None, tt, GROUP_WIDTH), lambda b, i: (b, i, 0)),
        out_shape=jax.ShapeDtypeStruct((bsz, s, GROUP_WIDTH), BF16),
        scratch_shapes=[pltpu.VMEM((n_pat, GROUP_HEADS, DIL_BLOCK, 2 * DIL_BLOCK), F32),
                        pltpu.VMEM((halves, 2 * tt, LANE), F32), pltpu.VMEM((halves, 2 * tt, LANE), F32),
                        pltpu.VMEM((n_pat, halves, tt, LANE), F32),
                        pltpu.VMEM((n_pat, halves, tt, LANE), F32)],
        compiler_params=_params(2), name="dilated_attn",
    )(rel_bias, q, k, v)


def _causal_conv(x, xbuf, cw_ref, cb_ref, first_tile):
    t = x.shape[0]

    @pl.when(first_tile)
    def _():
        xbuf[0:CONV_PAD, :] = jnp.zeros((CONV_PAD, x.shape[1]), F32)

    xbuf[CONV_PAD:CONV_PAD + t, :] = x
    y = cb_ref[...] + cw_ref[CONV_WIDTH - 1:CONV_WIDTH, :] * x
    for kk in range(CONV_WIDTH - 1):
        off = CONV_PAD - (CONV_WIDTH - 1) + kk
        y = y + cw_ref[kk:kk + 1, :] * xbuf[off:off + t, :]
    xbuf[0:CONV_PAD, :] = x[t - CONV_PAD:t, :]
    return y


def _lru_kernel(g_ref, x_ref, cw_ref, cb_ref, wa_ref, ba_ref, wx_ref, bx_ref, lam_ref, o_ref,
                xbuf, a_scr, b_scr, h_scr, hcar):
    first_tile = pl.program_id(1) == 0
    ts = x_ref.shape[0]

    @pl.when(first_tile)
    def _():
        hcar[...] = jnp.zeros_like(hcar)

    xc = _causal_conv(x_ref[...], xbuf, cw_ref, cb_ref, first_tile)
    xcb = xc.astype(BF16)
    r = jax.nn.sigmoid(jnp.dot(xcb, wa_ref[...], preferred_element_type=F32) + ba_ref[...])
    i = jax.nn.sigmoid(jnp.dot(xcb, wx_ref[...], preferred_element_type=F32) + bx_ref[...])
    neg_lam = -lam_ref[...]
    softplus = jnp.maximum(neg_lam, 0.0) + jnp.log1p(jnp.exp(-jnp.abs(neg_lam)))
    log_a = -LRU_C * r * softplus
    a = jnp.exp(log_a)
    a_scr[...] = a
    b_scr[...] = jnp.sqrt(-jnp.tanh(log_a) * (a * a + 1.0)) * (i * xc)

    row = lax.broadcasted_iota(jnp.int32, (SUBLANE, GROUP_WIDTH), 0)

    def body(j, hprev):
        r0 = pl.multiple_of(j * SUBLANE, SUBLANE)
        a = a_scr[pl.ds(r0, SUBLANE), :]
        b = b_scr[pl.ds(r0, SUBLANE), :]
        for d in (1, 2, 4):
            a_sh = jnp.where(row >= d, pltpu.roll(a, d, 0), 1.0)
            b_sh = jnp.where(row >= d, pltpu.roll(b, d, 0), 0.0)
            b = a * b_sh + b
            a = a * a_sh
        h_scr[pl.ds(r0, SUBLANE), :] = a * hprev + b
        a_last = jnp.broadcast_to(a[SUBLANE - 1:SUBLANE, :], a.shape)
        b_last = jnp.broadcast_to(b[SUBLANE - 1:SUBLANE, :], b.shape)
        return a_last * hprev + b_last

    hcar[...] = lax.fori_loop(0, ts // SUBLANE, body, hcar[...], unroll=8)
    o_ref[...] = (jax.nn.gelu(g_ref[...], approximate=True) * h_scr[...]).astype(o_ref.dtype)


def _rg_lru(gate, x, conv_w, conv_b, wa, ba, wx, bx, lam, ts=512):
    bsz, s, w = x.shape
    row = pl.BlockSpec((None, ts, w), lambda b, i: (b, i, 0))
    vec = _full((1, w))
    return pl.pallas_call(
        _lru_kernel, grid=(bsz, s // ts),
        in_specs=[row, row, _full((CONV_WIDTH, w)), vec, _full((w, w)), vec, _full((w, w)), vec, vec],
        out_specs=row, out_shape=jax.ShapeDtypeStruct((bsz, s, w), BF16),
        scratch_shapes=[pltpu.VMEM((ts + CONV_PAD, w), F32), pltpu.VMEM((ts, w), F32),
                        pltpu.VMEM((ts, w), F32), pltpu.VMEM((ts, w), F32), pltpu.VMEM((SUBLANE, w), F32)],
        compiler_params=_params(2), name="rg_lru",
    )(gate, x, conv_w, conv_b, wa, ba, wx, bx, lam)


def _ssd_kernel(z_ref, xbc_ref, dt_ref, cw_ref, cb_ref, dtb_ref, alog_ref, dsk_ref, ng_ref, o_ref,
                xbuf, st):
    first_tile = pl.program_id(1) == 0
    t = SSM_CHUNK
    gl = GROUP_WIDTH // SSM_GROUPS

    @pl.when(first_tile)
    def _():
        st[...] = jnp.zeros_like(st)

    xc_all = _causal_conv(xbc_ref[...], xbuf, cw_ref, cb_ref, first_tile)
    xc_all = xc_all * jax.nn.sigmoid(xc_all)
    dt_in = dt_ref[...] + dtb_ref[...]
    dtl_all = jnp.maximum(dt_in, 0.0) + jnp.log1p(jnp.exp(-jnp.abs(dt_in)))
    adt_all = dtl_all * (-jnp.exp(alog_ref[...]))
    li = lax.broadcasted_iota(jnp.int32, (t, t), 0)
    si = lax.broadcasted_iota(jnp.int32, (t, t), 1)
    causal = li >= si
    tri = causal.astype(F32)
    low = lax.broadcasted_iota(jnp.int32, (1, gl), 1) < HEAD_DIM

    def lanes(col):
        return jnp.broadcast_to(col, (t, LANE))

    for cc in range(xbc_ref.shape[0] // t):
        r0 = cc * t
        xc = xc_all[r0:r0 + t]
        dtl = dtl_all[r0:r0 + t]
        acum = jnp.dot(tri, adt_all[r0:r0 + t], precision=HIGHEST, preferred_element_type=F32)
        for g in range(SSM_GROUPS):
            xg = xc[:, g * gl:(g + 1) * gl]
            bg = xc[:, GROUP_WIDTH + g * SSM_STATE:GROUP_WIDTH + (g + 1) * SSM_STATE]
            cg = xc[:, GROUP_WIDTH + (SSM_GROUPS + g) * SSM_STATE:GROUP_WIDTH + (SSM_GROUPS + g + 1) * SSM_STATE]
            h0, h1 = 2 * g, 2 * g + 1
            xdt = (xg * jnp.where(low, lanes(dtl[:, h0:h0 + 1]), lanes(dtl[:, h1:h1 + 1]))).astype(BF16)
            cgb = cg.astype(BF16)
            cb = lax.dot_general(cgb, bg.astype(BF16), (((1,), (1,)), ((), ())), preferred_element_type=F32)
            state = st[g]
            y_off = jnp.dot(cgb, state.astype(BF16), preferred_element_type=F32)
            y_dg, st_new, ea, cdec = [], [], [], []
            for h in (h0, h1):
                ac = lanes(acum[:, h:h + 1])
                lmat = jnp.exp(jnp.where(causal, ac - ac.T, -jnp.inf))
                y_dg.append(jnp.dot((cb * lmat).astype(BF16), xdt, preferred_element_type=F32))
                a_last = ac[t - 1:t, :]
                bdec = bg * jnp.exp(a_last - ac)
                st_new.append(jnp.dot(bdec.T.astype(BF16), xdt, preferred_element_type=F32))
                ea.append(jnp.exp(ac))
                cdec.append(jnp.exp(a_last))
            y = (jnp.where(low, y_dg[0], y_dg[1]) + y_off * jnp.where(low, ea[0], ea[1])
                 + xg * dsk_ref[:, g * gl:(g + 1) * gl])
            st[g] = state * jnp.where(low, cdec[0], cdec[1]) + jnp.where(low, st_new[0], st_new[1])
            zg = z_ref[r0:r0 + t, g * gl:(g + 1) * gl]
            y = y * (zg * jax.nn.sigmoid(zg))
            o_ref[r0:r0 + t, g * gl:(g + 1) * gl] = _rms(y, ng_ref[:, g * gl:(g + 1) * gl]).astype(o_ref.dtype)


def _mamba2_ssd(z, xbc, dt, conv_w, conv_b, dt_bias, a_log, d_skip, norm_gain, chunks=2):
    bsz, s, w = z.shape
    ts = chunks * SSM_CHUNK
    row = lambda width: pl.BlockSpec((None, ts, width), lambda b, i: (b, i, 0))
    return pl.pallas_call(
        _ssd_kernel, grid=(bsz, s // ts),
        in_specs=[row(w), row(SSM_CONV_DIM), row(LANE), _full((CONV_WIDTH, SSM_CONV_DIM)),
                  _full((1, SSM_CONV_DIM)), _full((1, LANE)), _full((1, LANE)),
                  _full((1, w)), _full((1, w))],
        out_specs=row(w), out_shape=jax.ShapeDtypeStruct((bsz, s, w), BF16),
        scratch_shapes=[pltpu.VMEM((ts + CONV_PAD, SSM_CONV_DIM), F32),
                        pltpu.VMEM((SSM_GROUPS, SSM_STATE, GROUP_WIDTH // SSM_GROUPS), F32)],
        compiler_params=_params(2), name="ssd",
    )(z, xbc, dt, conv_w, conv_b, dt_bias, a_log, d_skip, norm_gain)


def _diff_kernel(tab_ref, qt_ref, k_ref, vt_ref, lq1, lk1, lq2, lk2, gain_ref, o_ref,
                 bias_scr, qm_scr, s_a, s_b, s_c, mc_a, mc_b, mc_c, m_scr, acc_scr, ot_scr, *, lam_init):
    tq = DIFF_TQ
    qi = pl.program_id(1)
    n_sm = 2 * GROUP_HEADS
    grp = tq // SUBLANE
    acc_grp = DIFF_ACC_ROWS // SUBLANE
    s_bufs, mc_bufs = (s_a, s_b, s_c), (mc_a, mc_b, mc_c)
    nbuf = len(s_bufs)

    @pl.when((pl.program_id(0) == 0) & (qi == 0))
    def _():
        ki_ = lax.broadcasted_iota(jnp.int32, (tq, tq), 0)
        qi_ = lax.broadcasted_iota(jnp.int32, (tq, tq), 1)
        for h in range(GROUP_HEADS):
            for d in range(DIFF_NEAR):
                dist = d * tq + qi_ - ki_
                bias = _bias_from_dist(jnp.maximum(dist, 0), tab_ref, GROUP_HEADS + h) * LOG2E
                bias_scr[d, h] = jnp.where(dist >= 0, bias, -jnp.inf)
            bias_scr[DIFF_NEAR, h] = jnp.full((tq, tq), tab_ref[NUM_BUCKETS - 1, GROUP_HEADS + h], F32) * LOG2E
            bias_scr[DIFF_NEAR + 1, h] = jnp.full((tq, tq), -jnp.inf, F32)

    qt = qt_ref[...]
    feat = lax.broadcasted_iota(jnp.int32, (GROUP_WIDTH, 1), 0)
    for idx in range(n_sm):
        qm_scr[idx] = jnp.where((feat // DIFF_QK_DIM) == idx, qt, jnp.zeros_like(qt))
    m_scr[...] = jnp.full(m_scr.shape, -1e30, F32)
    acc_scr[...] = jnp.zeros_like(acc_scr)

    c_far = [jnp.full((SUBLANE, tq), tab_ref[NUM_BUCKETS - 1, GROUP_HEADS + h], F32) * LOG2E
             for h in range(GROUP_HEADS)]
    ones_rows = jnp.ones((DIFF_ACC_ROWS - HEAD_DIM, tq), BF16)
    n_far = jnp.maximum(qi + 1 - DIFF_NEAR, 0)
    j_far = lax.div(jnp.maximum(n_far - 1, 0), nbuf)
    n_raw = nbuf * j_far

    def rows_max(x3):
        part = jnp.max(x3, axis=0)
        return jnp.broadcast_to(jnp.max(part, axis=0, keepdims=True), part.shape)

    def rows_sum(x3):
        part = jnp.sum(x3, axis=0)
        return jnp.broadcast_to(jnp.sum(part, axis=0, keepdims=True), part.shape)

    def scores(ki, s_buf):
        kb = k_ref[pl.ds(pl.multiple_of(ki * tq, tq), tq), :]
        for idx in range(n_sm):
            s_buf[idx] = jnp.dot(kb, qm_scr[idx], preferred_element_type=F32)

    def max_any(ki, s_buf, mc):
        d = qi - ki
        tile = jnp.where(d < 0, DIFF_NEAR + 1, jnp.minimum(d, DIFF_NEAR))
        for idx in range(n_sm):
            t = s_buf[idx] + bias_scr[tile, idx // 2]
            s_buf[idx] = t
            mc[idx] = rows_max(t.reshape(grp, SUBLANE, tq))

    def max_far(ki, s_buf, mc):
        for idx in range(n_sm):
            mc[idx] = rows_max(s_buf[idx].reshape(grp, SUBLANE, tq)) + c_far[idx // 2]

    def exp_pass(ki, s_buf, mc):
        vt = vt_ref[jnp.minimum(ki, qi)]
        raw = (ki >= 1) & (ki <= n_raw)
        for idx in range(n_sm):
            h = idx // 2
            m_prev = m_scr[idx]
            m_next = jnp.maximum(m_prev, mc[idx])
            shift = m_next - jnp.where(raw, c_far[h], 0.0)
            p3 = jnp.exp2(s_buf[idx].reshape(grp, SUBLANE, tq) - shift[None])
            alpha = jnp.exp2(m_prev - m_next)
            lhs = jnp.concatenate([vt[h * HEAD_DIM:(h + 1) * HEAD_DIM, :], ones_rows], axis=0)
            pv = jnp.dot(lhs, p3.reshape(tq, tq).astype(BF16), preferred_element_type=F32)
            acc = acc_scr[idx].reshape(acc_grp, SUBLANE, tq) * alpha[None]
            acc_scr[idx] = acc.reshape(DIFF_ACC_ROWS, tq) + pv
            m_scr[idx] = m_next

    def trip(j, carry, max_pass):
        for u in range(nbuf):
            b = nbuf * j + u
            scores(jnp.minimum(b + 2, qi), s_bufs[(u + 2) % nbuf])
            max_pass(b + 1, s_bufs[(u + 1) % nbuf], mc_bufs[(u + 1) % nbuf])
            exp_pass(b, s_bufs[u], mc_bufs[u])
        return carry

    scores(0, s_bufs[0])
    scores(jnp.minimum(1, qi), s_bufs[1])
    max_any(0, s_bufs[0], mc_bufs[0])
    lax.fori_loop(0, j_far, functools.partial(trip, max_pass=max_far), 0)
    lax.fori_loop(j_far, lax.div(qi + nbuf, nbuf), functools.partial(trip, max_pass=max_any), 0)

    lam = (jnp.exp(jnp.sum(lq1[...] * lk1[...], axis=1, keepdims=True))
           - jnp.exp(jnp.sum(lq2[...] * lk2[...], axis=1, keepdims=True)) + lam_init)
    for h in range(GROUP_HEADS):
        shape3 = (HEAD_DIM // SUBLANE, SUBLANE, tq)
        a1, a2 = acc_scr[2 * h], acc_scr[2 * h + 1]
        o1 = a1[0:HEAD_DIM].reshape(shape3) / a1[HEAD_DIM:HEAD_DIM + SUBLANE][None]
        o2 = a2[0:HEAD_DIM].reshape(shape3) / a2[HEAD_DIM:HEAD_DIM + SUBLANE][None]
        oh = o1 - lam * o2
        ms = rows_sum(oh * oh) * (1.0 / HEAD_DIM)
        oh = oh * lax.rsqrt(ms + NORM_EPS)[None]
        ot_scr[h * HEAD_DIM:(h + 1) * HEAD_DIM, :] = oh.reshape(HEAD_DIM, tq)
    o_ref[...] = ((ot_scr[...].T * gain_ref[...]) * (1.0 - lam_init)).astype(o_ref.dtype)


def _diff_attention(qt, k, vt, rel_bias, lq1, lk1, lq2, lk2, gain, lam_init):
    bsz, s, gw = k.shape
    tq = DIFF_TQ
    n_sm = 2 * GROUP_HEADS
    lvec = _full((1, DIFF_QK_DIM))
    stat = pltpu.VMEM((n_sm, SUBLANE, tq), F32)
    return pl.pallas_call(
        functools.partial(_diff_kernel, lam_init=lam_init), grid=(bsz, s // tq),
        in_specs=[pl.BlockSpec(memory_space=pltpu.SMEM),
                  pl.BlockSpec((None, None, gw, tq), lambda b, i: (b, i, 0, 0)),
                  pl.BlockSpec((None, s, gw), lambda b, i: (b, 0, 0)),
                  pl.BlockSpec((None, s // tq, gw, tq), lambda b, i: (b, 0, 0, 0)),
                  lvec, lvec, lvec, lvec, _full((1, gw))],
        out_specs=pl.BlockSpec((None, tq, gw), lambda b, i: (b, i, 0)),
        out_shape=jax.ShapeDtypeStruct((bsz, s, gw), BF16),
        scratch_shapes=[pltpu.VMEM((DIFF_NEAR + 2, GROUP_HEADS, tq, tq), F32),
                        pltpu.VMEM((n_sm, gw, tq), BF16),
                        pltpu.VMEM((n_sm, tq, tq), F32), pltpu.VMEM((n_sm, tq, tq), F32),
                        pltpu.VMEM((n_sm, tq, tq), F32), stat, stat, stat, stat,
                        pltpu.VMEM((n_sm, DIFF_ACC_ROWS, tq), F32),
                        pltpu.VMEM((gw, tq), F32)],
        compiler_params=_params(2), name="diff_attn",
    )(rel_bias, qt, k, vt, lq1, lk1, lq2, lk2, gain)


def _mix_ffn_kernel(h_ref, ya, yb, yc, yd, wo_ref, g_mix, g_pre, wu_ref, wd_ref, g_post, o_ref, *, chunk):
    gw = GROUP_WIDTH
    acc = jnp.dot(ya[...], wo_ref[0:gw, :], preferred_element_type=F32)
    for j, y in enumerate((yb, yc, yd), start=1):
        acc = acc + jnp.dot(y[...], wo_ref[j * gw:(j + 1) * gw, :], preferred_element_type=F32)
    x = h_ref[...] + _rms(acc, g_mix[...])
    u = _rms(x, g_pre[...]).astype(BF16)
    acc = jnp.zeros(x.shape, F32)
    for c in range(wu_ref.shape[1] // chunk):
        f = jnp.dot(u, wu_ref[:, c * chunk:(c + 1) * chunk], preferred_element_type=F32)
        f = jnp.square(jnp.maximum(f, 0.0)).astype(BF16)
        acc = acc + jnp.dot(f, wd_ref[c * chunk:(c + 1) * chunk, :], preferred_element_type=F32)
    o_ref[...] = x + _rms(acc, g_post[...])


def _mix_ffn(h, ya, yb, yc, yd, w_out, g_mix, g_pre, w_up, w_down, g_post, tm=512, chunk=1024):
    bsz, s, d = h.shape
    row = lambda width: pl.BlockSpec((None, tm, width), lambda b, i: (b, i, 0))
    vec = _full((1, d))
    return pl.pallas_call(
        functools.partial(_mix_ffn_kernel, chunk=chunk), grid=(bsz, s // tm),
        in_specs=[row(d)] + [row(GROUP_WIDTH)] * 4 + [_resident(w_out.shape), vec, vec,
                                                      _resident(w_up.shape), _resident(w_down.shape), vec],
        out_specs=row(d), out_shape=jax.ShapeDtypeStruct(h.shape, F32), compiler_params=_params(2),
        name="mix_ffn",
    )(h, ya, yb, yc, yd, w_out, g_mix, g_pre, w_up, w_down, g_post)


def _block_diag(w):
    nb, n, _ = w.shape
    eye = jnp.eye(nb, dtype=w.dtype)
    return (eye[:, None, :, None] * w[:, :, None, :]).reshape(nb * n, nb * n)


def _pad_lanes(v):
    return jnp.pad(v.astype(F32), (0, LANE - v.shape[0]))[None, :]


def kernel(x, rel_bias, norm_mix_pre, norm_mix_post, norm_ffn_pre, norm_ffn_post, w_in, w_out,
           lru_conv_w, lru_conv_b, lru_wa, lru_ba, lru_wx, lru_bx, lru_lambda,
           ssm_conv_w, ssm_conv_b, ssm_dt_bias, ssm_a_log, ssm_d, ssm_norm,
           diff_lq1, diff_lk1, diff_lq2, diff_lk2, diff_norm, w_ff_up, w_ff_down):
    depth = w_in.shape[0]
    vec = lambda p: p.astype(F32)[None, :]
    h = x
    for layer in range(depth):
        w_all = _pack_w_in(w_in, layer)
        qa, ka, va, gb, xb, zc, xbc, qdt, kd, vdt, dt = _in_proj(h, vec(norm_mix_pre[layer]), w_all)

        ya = _dilated_attention(qa, ka, va, rel_bias)
        yb = _rg_lru(gb, xb, lru_conv_w[layer], vec(lru_conv_b[layer]),
                     _block_diag(lru_wa[layer]).astype(BF16), vec(lru_ba[layer]),
                     _block_diag(lru_wx[layer]).astype(BF16), vec(lru_bx[layer]), vec(lru_lambda[layer]))
        yc = _mamba2_ssd(zc, xbc, dt, ssm_conv_w[layer], vec(ssm_conv_b[layer]),
                         _pad_lanes(ssm_dt_bias[layer]), _pad_lanes(ssm_a_log[layer]),
                         jnp.repeat(ssm_d[layer].astype(F32), HEAD_DIM)[None, :], vec(ssm_norm[layer]))
        lam_init = 0.8 - 0.6 * math.exp(-0.3 * layer)
        yd = _diff_attention(qdt, kd, vdt, rel_bias, vec(diff_lq1[layer]), vec(diff_lk1[layer]),
                             vec(diff_lq2[layer]), vec(diff_lk2[layer]),
                             jnp.tile(diff_norm[layer].astype(F32), GROUP_HEADS)[None, :], lam_init)

        h = _mix_ffn(h, ya, yb, yc, yd, w_out[layer].astype(BF16), vec(norm_mix_post[layer]),
                     vec(norm_ffn_pre[layer]), w_ff_up[layer].astype(BF16), w_ff_down[layer].astype(BF16),
                     vec(norm_ffn_post[layer]))
    return h
```

```python
import functools
import math

import numpy as np
import jax
import jax.numpy as jnp
from jax import lax
from jax.experimental import pallas as pl
from jax.experimental.pallas import tpu as pltpu

F32 = jnp.float32
BF16 = jnp.bfloat16
HIGHEST = lax.Precision.HIGHEST

NORM_EPS = 1e-6
HEAD_DIM = 64
GROUP_HEADS = 4
GROUP_WIDTH = GROUP_HEADS * HEAD_DIM
NUM_BUCKETS = 32
MAX_DISTANCE = 2048
DILATED_PATTERNS = ((128, 1), (512, 4), (2048, 16))
DIL_BLOCK = 128
DIL_TILE = DIL_BLOCK * max(d for _, d in DILATED_PATTERNS)
LRU_C = 8.0
CONV_WIDTH = 4
SSM_GROUPS = 2
SSM_STATE = 128
SSM_CHUNK = 128
SSM_CONV_DIM = GROUP_WIDTH + 2 * SSM_GROUPS * SSM_STATE
DIFF_QK_DIM = HEAD_DIM // 2
DIFF_TQ = 256
DIFF_ACC_ROWS = HEAD_DIM + 16
LOG2E = math.log2(math.e)
LANE = 128
SUBLANE = 8
CONV_PAD = 8
VMEM_LIMIT = 48 * 1024 * 1024

_C_QA, _C_KA, _C_VA = 0, 256, 512
_C_GB, _C_XB = 768, 1024
_C_ZC, _C_XBC = 1280, 1536
_C_QD, _C_KD, _C_VD = 2304, 2560, 2816
_C_DT = 3072
_C_END = 3200
_W_IN_DT = 3 * GROUP_WIDTH + 2 * GROUP_WIDTH + GROUP_WIDTH + SSM_CONV_DIM
SSM_HEADS = GROUP_HEADS


def _t5_thresholds():
    n = np.arange(1, 4 * MAX_DISTANCE)
    max_exact = NUM_BUCKETS // 2
    large = max_exact + (np.log(n / max_exact) / math.log(MAX_DISTANCE / max_exact)
                         * (NUM_BUCKETS - max_exact)).astype(np.int64)
    bucket = np.where(n < max_exact, n, np.minimum(large, NUM_BUCKETS - 1))
    return tuple(int(n[bucket >= b].min()) for b in range(1, NUM_BUCKETS))


_T5_THR = _t5_thresholds()
DIFF_NEAR = -(-(_T5_THR[-1] + DIFF_TQ - 1) // DIFF_TQ)


def _bias_from_dist(dist, tab_ref, col):
    out = jnp.full(dist.shape, tab_ref[0, col], F32)
    for b in range(1, NUM_BUCKETS):
        out = jnp.where(dist >= _T5_THR[b - 1], tab_ref[b, col], out)
    return out


def _rms(x, g):
    return x * lax.rsqrt(jnp.mean(x * x, axis=-1, keepdims=True) + NORM_EPS) * g


def _params(n_axes):
    return pltpu.CompilerParams(dimension_semantics=("arbitrary",) * n_axes,
                                vmem_limit_bytes=VMEM_LIMIT)


def _full(shape):
    return pl.BlockSpec(shape, lambda *_: (0,) * len(shape))


def _resident(shape):
    return pl.BlockSpec(shape, lambda *_: (0,) * len(shape), pipeline_mode=pl.Buffered(1))


def _pack_w_in_kernel(w_ref, o_ref):
    d_cols = _C_DT - _W_IN_DT
    o_ref[:, 0:_W_IN_DT] = w_ref[:, 0:_W_IN_DT].astype(BF16)
    o_ref[:, _W_IN_DT:_C_DT] = w_ref[:, _W_IN_DT + SSM_HEADS:_W_IN_DT + SSM_HEADS + d_cols].astype(BF16)
    lane = lax.broadcasted_iota(jnp.int32, (1, LANE), 1)
    dt_tile = w_ref[:, _W_IN_DT:_W_IN_DT + LANE]
    o_ref[:, _C_DT:_C_END] = jnp.where(lane < SSM_HEADS, dt_tile, 0.0).astype(BF16)


def _pack_w_in(w_in, layer, tr=256):
    _, d, p_in = w_in.shape
    assert p_in == _C_DT + SSM_HEADS
    return pl.pallas_call(
        _pack_w_in_kernel, grid=(d // tr,),
        in_specs=[pl.BlockSpec((None, tr, p_in), lambda i: (layer, i, 0))],
        out_specs=pl.BlockSpec((tr, _C_END), lambda i: (i, 0)),
        out_shape=jax.ShapeDtypeStruct((d, _C_END), BF16), compiler_params=_params(1), name="pack_w_in",
    )(w_in)


def _in_proj_kernel(h_ref, g_ref, w_ref, qa, ka, va, gb, xb, zc, xbc, qdt, kd, vdt, dt):
    u = _rms(h_ref[...], g_ref[...]).astype(BF16)

    def seg(lo, hi):
        return jnp.dot(u, w_ref[:, lo:hi], preferred_element_type=F32)

    def store_transposed(ref, val):
        for j in range(ref.shape[0]):
            ref[j] = val[j * DIFF_TQ:(j + 1) * DIFF_TQ, :].T.astype(BF16)

    def store_halves(ref, val):
        for hh in range(ref.shape[0]):
            ref[hh] = val[:, hh * LANE:(hh + 1) * LANE]

    store_halves(qa, seg(_C_QA, _C_KA) * (HEAD_DIM ** -0.5))
    store_halves(ka, seg(_C_KA, _C_VA))
    store_halves(va, seg(_C_VA, _C_GB))
    gb[...] = seg(_C_GB, _C_XB)
    xb[...] = seg(_C_XB, _C_ZC)
    zc[...] = seg(_C_ZC, _C_XBC)
    xbc[...] = seg(_C_XBC, _C_QD)
    store_transposed(qdt, seg(_C_QD, _C_KD) * (DIFF_QK_DIM ** -0.5 * LOG2E))
    kd[...] = seg(_C_KD, _C_VD).astype(BF16)
    store_transposed(vdt, seg(_C_VD, _C_DT))
    dt[...] = seg(_C_DT, _C_END)


def _in_proj(h, gain, w_all, tm=512):
    bsz, s, d = h.shape
    gw = GROUP_WIDTH
    row = lambda width: pl.BlockSpec((None, tm, width), lambda b, i: (b, i, 0))
    shp = lambda width, dt: jax.ShapeDtypeStruct((bsz, s, width), dt)
    tshape = jax.ShapeDtypeStruct((bsz, s // DIFF_TQ, gw, DIFF_TQ), BF16)
    tspec = pl.BlockSpec((None, tm // DIFF_TQ, gw, DIFF_TQ), lambda b, i: (b, i, 0, 0))
    hshape = jax.ShapeDtypeStruct((bsz, gw // LANE, s, LANE), F32)
    hspec = pl.BlockSpec((None, gw // LANE, tm, LANE), lambda b, i: (b, 0, i, 0))
    out_shape = (hshape, hshape, hshape, shp(gw, F32), shp(gw, F32),
                 shp(gw, F32), shp(SSM_CONV_DIM, F32), tshape, shp(gw, BF16), tshape, shp(LANE, F32))
    out_specs = (hspec, hspec, hspec, row(gw), row(gw), row(gw), row(SSM_CONV_DIM),
                 tspec, row(gw), tspec, row(LANE))
    return pl.pallas_call(
        _in_proj_kernel, grid=(bsz, s // tm),
        in_specs=[row(d), _full((1, d)), _resident(w_all.shape)],
        out_specs=out_specs, out_shape=out_shape, compiler_params=_params(2), name="in_proj",
    )(h, gain, w_all)


def _dil_kernel(tab_ref, q_ref, k_ref, v_ref, out_ref, bias_scr, kbuf, vbuf, o_scr, lse_scr):
    tile = pl.program_id(1)
    tt = DIL_TILE
    nq, nk = DIL_BLOCK, 2 * DIL_BLOCK
    halves = GROUP_WIDTH // LANE
    heads_per_half = LANE // HEAD_DIM

    @pl.when((pl.program_id(0) == 0) & (tile == 0))
    def _():
        qi = lax.broadcasted_iota(jnp.int32, (nq, nk), 0)
        kj = lax.broadcasted_iota(jnp.int32, (nq, nk), 1)
        rel = qi + DIL_BLOCK - kj
        for p, (window, dil) in enumerate(DILATED_PATTERNS):
            valid = (rel >= 0) & (rel <= window // dil)
            dist = jnp.maximum(rel, 0) * dil
            for h in range(GROUP_HEADS):
                bias_scr[p, h] = jnp.where(valid, _bias_from_dist(dist, tab_ref, h), -jnp.inf)

    @pl.when(tile == 0)
    def _():
        kbuf[:, 0:tt, :] = jnp.zeros((halves, tt, LANE), F32)
        vbuf[:, 0:tt, :] = jnp.zeros((halves, tt, LANE), F32)

    kbuf[:, tt:2 * tt, :] = k_ref[...]
    vbuf[:, tt:2 * tt, :] = v_ref[...]

    head_in_half = lax.broadcasted_iota(jnp.int32, (1, LANE), 1) // HEAD_DIM
    kj = lax.broadcasted_iota(jnp.int32, (1, nk), 1)

    for p, (_, dil) in enumerate(DILATED_PATTERNS):
        nbt = tt // (DIL_BLOCK * dil)

        def body(j, carry, p=p, dil=dil, nbt=nbt):
            r, n = j // nbt, j % nbt
            start = n * (DIL_BLOCK * dil) + r
            kstart = start + tt - DIL_BLOCK * dil
            if dil == 1:
                rows = pl.ds(pl.multiple_of(start, DIL_BLOCK), nq)
                krows = pl.ds(pl.multiple_of(kstart, DIL_BLOCK), nk)
            else:
                rows = pl.ds(start, nq, stride=dil)
                krows = pl.ds(kstart, nk, stride=dil)
            key_ok = kj >= jnp.where(tile * nbt + n > 0, 0, DIL_BLOCK)
            scores = []
            for hh in range(halves):
                q = q_ref[hh, rows, :].astype(BF16)
                k = kbuf[hh, krows, :].astype(BF16)
                for hr in range(heads_per_half):
                    qh = jnp.where(head_in_half == hr, q, jnp.zeros_like(q))
                    scores.append(lax.dot_general(qh, k, (((1,), (1,)), ((), ())), preferred_element_type=F32))
            for hh in range(halves):
                v = vbuf[hh, krows, :].astype(BF16)
                o_acc = jnp.zeros((nq, LANE), F32)
                lse_acc = jnp.zeros((nq, LANE), F32)
                for hr in range(heads_per_half):
                    hm = head_in_half == hr
                    h = hh * heads_per_half + hr
                    sc = jnp.where(key_ok, scores[h] + bias_scr[p, h], -jnp.inf)
                    m = jnp.max(sc, axis=-1, keepdims=True)
                    e = jnp.exp(sc - m)
                    den = jnp.sum(e, axis=-1, keepdims=True)
                    oh = jnp.dot(e.astype(BF16), v, preferred_element_type=F32) / den
                    o_acc = jnp.where(hm, oh, o_acc)
                    lse_acc = jnp.where(hm, m + jnp.log(den), lse_acc)
                o_scr[p, hh, rows, :] = o_acc
                lse_scr[p, hh, rows, :] = lse_acc
            return carry

        lax.fori_loop(0, tt // DIL_BLOCK, body, 0, unroll=8)

    cm = 256

    def combine(c, carry):
        rows = pl.ds(pl.multiple_of(c * cm, cm), cm)
        for hh in range(halves):
            l0, l1, l2 = lse_scr[0, hh, rows, :], lse_scr[1, hh, rows, :], lse_scr[2, hh, rows, :]
            m = jnp.maximum(jnp.maximum(l0, l1), l2)
            w0, w1, w2 = jnp.exp(l0 - m), jnp.exp(l1 - m), jnp.exp(l2 - m)
            num = w0 * o_scr[0, hh, rows, :] + w1 * o_scr[1, hh, rows, :] + w2 * o_scr[2, hh, rows, :]
            out_ref[rows, hh * LANE:(hh + 1) * LANE] = (num / (w0 + w1 + w2)).astype(out_ref.dtype)
        return carry

    lax.fori_loop(0, tt // cm, combine, 0)
    kbuf[:, 0:tt, :] = k_ref[...]
    vbuf[:, 0:tt, :] = v_ref[...]


def _dilated_attention(q, k, v, rel_bias):
    bsz, halves, s, _ = q.shape
    tt = DIL_TILE
    n_pat = len(DILATED_PATTERNS)
    assert s % tt == 0 and n_pat == 3
    hspec = pl.BlockSpec((None, halves, tt, LANE), lambda b, i: (b, 0, i, 0))
    return pl.pallas_call(
        _dil_kernel, grid=(bsz, s // tt),
        in_specs=[pl.BlockSpec(memory_space=pltpu.SMEM), hspec, hspec, hspec],
        out_specs=pl.BlockSpec((None, tt, GROUP_WIDTH), lambda b, i: (b, i, 0)),
        out_shape=jax.ShapeDtypeStruct((bsz, s, GROUP_WIDTH), BF16),
        scratch_shapes=[pltpu.VMEM((n_pat, GROUP_HEADS, DIL_BLOCK, 2 * DIL_BLOCK), F32),
                        pltpu.VMEM((halves, 2 * tt, LANE), F32), pltpu.VMEM((halves, 2 * tt, LANE), F32),
                        pltpu.VMEM((n_pat, halves, tt, LANE), F32),
                        pltpu.VMEM((n_pat, halves, tt, LANE), F32)],
        compiler_params=_params(2), name="dilated_attn",
    )(rel_bias, q, k, v)


def _causal_conv(x, xbuf, cw_ref, cb_ref, first_tile):
    t = x.shape[0]

    @pl.when(first_tile)
    def _():
        xbuf[0:CONV_PAD, :] = jnp.zeros((CONV_PAD, x.shape[1]), F32)

    xbuf[CONV_PAD:CONV_PAD + t, :] = x
    y = cb_ref[...] + cw_ref[CONV_WIDTH - 1:CONV_WIDTH, :] * x
    for kk in range(CONV_WIDTH - 1):
        off = CONV_PAD - (CONV_WIDTH - 1) + kk
        y = y + cw_ref[kk:kk + 1, :] * xbuf[off:off + t, :]
    xbuf[0:CONV_PAD, :] = x[t - CONV_PAD:t, :]
    return y


def _lru_kernel(g_ref, x_ref, cw_ref, cb_ref, wa_ref, ba_ref, wx_ref, bx_ref, lam_ref, o_ref,
                xbuf, a_scr, b_scr, h_scr, hcar):
    first_tile = pl.program_id(1) == 0
    ts = x_ref.shape[0]

    @pl.when(first_tile)
    def _():
        hcar[...] = jnp.zeros_like(hcar)

    xc = _causal_conv(x_ref[...], xbuf, cw_ref, cb_ref, first_tile)
    xcb = xc.astype(BF16)
    r = jax.nn.sigmoid(jnp.dot(xcb, wa_ref[...], preferred_element_type=F32) + ba_ref[...])
    i = jax.nn.sigmoid(jnp.dot(xcb, wx_ref[...], preferred_element_type=F32) + bx_ref[...])
    neg_lam = -lam_ref[...]
    softplus = jnp.maximum(neg_lam, 0.0) + jnp.log1p(jnp.exp(-jnp.abs(neg_lam)))
    log_a = -LRU_C * r * softplus
    a = jnp.exp(log_a)
    a_scr[...] = a
    b_scr[...] = jnp.sqrt(-jnp.tanh(log_a) * (a * a + 1.0)) * (i * xc)

    row = lax.broadcasted_iota(jnp.int32, (SUBLANE, GROUP_WIDTH), 0)

    def body(j, hprev):
        r0 = pl.multiple_of(j * SUBLANE, SUBLANE)
        a = a_scr[pl.ds(r0, SUBLANE), :]
        b = b_scr[pl.ds(r0, SUBLANE), :]
        for d in (1, 2, 4):
            a_sh = jnp.where(row >= d, pltpu.roll(a, d, 0), 1.0)
            b_sh = jnp.where(row >= d, pltpu.roll(b, d, 0), 0.0)
            b = a * b_sh + b
            a = a * a_sh
        h_scr[pl.ds(r0, SUBLANE), :] = a * hprev + b
        a_last = jnp.broadcast_to(a[SUBLANE - 1:SUBLANE, :], a.shape)
        b_last = jnp.broadcast_to(b[SUBLANE - 1:SUBLANE, :], b.shape)
        return a_last * hprev + b_last

    hcar[...] = lax.fori_loop(0, ts // SUBLANE, body, hcar[...], unroll=8)
    o_ref[...] = (jax.nn.gelu(g_ref[...], approximate=True) * h_scr[...]).astype(o_ref.dtype)


def _rg_lru(gate, x, conv_w, conv_b, wa, ba, wx, bx, lam, ts=512):
    bsz, s, w = x.shape
    row = pl.BlockSpec((None, ts, w), lambda b, i: (b, i, 0))
    vec = _full((1, w))
    return pl.pallas_call(
        _lru_kernel, grid=(bsz, s // ts),
        in_specs=[row, row, _full((CONV_WIDTH, w)), vec, _full((w, w)), vec, _full((w, w)), vec, vec],
        out_specs=row, out_shape=jax.ShapeDtypeStruct((bsz, s, w), BF16),
        scratch_shapes=[pltpu.VMEM((ts + CONV_PAD, w), F32), pltpu.VMEM((ts, w), F32),
                        pltpu.VMEM((ts, w), F32), pltpu.VMEM((ts, w), F32), pltpu.VMEM((SUBLANE, w), F32)],
        compiler_params=_params(2), name="rg_lru",
    )(gate, x, conv_w, conv_b, wa, ba, wx, bx, lam)


def _ssd_kernel(z_ref, xbc_ref, dt_ref, cw_ref, cb_ref, dtb_ref, alog_ref, dsk_ref, ng_ref, o_ref,
                xbuf, st):
    first_tile = pl.program_id(1) == 0
    t = SSM_CHUNK
    gl = GROUP_WIDTH // SSM_GROUPS

    @pl.when(first_tile)
    def _():
        st[...] = jnp.zeros_like(st)

    xc_all = _causal_conv(xbc_ref[...], xbuf, cw_ref, cb_ref, first_tile)
    xc_all = xc_all * jax.nn.sigmoid(xc_all)
    dt_in = dt_ref[...] + dtb_ref[...]
    dtl_all = jnp.maximum(dt_in, 0.0) + jnp.log1p(jnp.exp(-jnp.abs(dt_in)))
    adt_all = dtl_all * (-jnp.exp(alog_ref[...]))
    li = lax.broadcasted_iota(jnp.int32, (t, t), 0)
    si = lax.broadcasted_iota(jnp.int32, (t, t), 1)
    causal = li >= si
    tri = causal.astype(F32)
    low = lax.broadcasted_iota(jnp.int32, (1, gl), 1) < HEAD_DIM

    def lanes(col):
        return jnp.broadcast_to(col, (t, LANE))

    for cc in range(xbc_ref.shape[0] // t):
        r0 = cc * t
        xc = xc_all[r0:r0 + t]
        dtl = dtl_all[r0:r0 + t]
        acum = jnp.dot(tri, adt_all[r0:r0 + t], precision=HIGHEST, preferred_element_type=F32)
        for g in range(SSM_GROUPS):
            xg = xc[:, g * gl:(g + 1) * gl]
            bg = xc[:, GROUP_WIDTH + g * SSM_STATE:GROUP_WIDTH + (g + 1) * SSM_STATE]
            cg = xc[:, GROUP_WIDTH + (SSM_GROUPS + g) * SSM_STATE:GROUP_WIDTH + (SSM_GROUPS + g + 1) * SSM_STATE]
            h0, h1 = 2 * g, 2 * g + 1
            xdt = (xg * jnp.where(low, lanes(dtl[:, h0:h0 + 1]), lanes(dtl[:, h1:h1 + 1]))).astype(BF16)
            cgb = cg.astype(BF16)
            cb = lax.dot_general(cgb, bg.astype(BF16), (((1,), (1,)), ((), ())), preferred_element_type=F32)
            state = st[g]
            y_off = jnp.dot(cgb, state.astype(BF16), preferred_element_type=F32)
            y_dg, st_new, ea, cdec = [], [], [], []
            for h in (h0, h1):
                ac = lanes(acum[:, h:h + 1])
                lmat = jnp.exp(jnp.where(causal, ac - ac.T, -jnp.inf))
                y_dg.append(jnp.dot((cb * lmat).astype(BF16), xdt, preferred_element_type=F32))
                a_last = ac[t - 1:t, :]
                bdec = bg * jnp.exp(a_last - ac)
                st_new.append(jnp.dot(bdec.T.astype(BF16), xdt, preferred_element_type=F32))
                ea.append(jnp.exp(ac))
                cdec.append(jnp.exp(a_last))
            y = (jnp.where(low, y_dg[0], y_dg[1]) + y_off * jnp.where(low, ea[0], ea[1])
                 + xg * dsk_ref[:, g * gl:(g + 1) * gl])
            st[g] = state * jnp.where(low, cdec[0], cdec[1]) + jnp.where(low, st_new[0], st_new[1])
            zg = z_ref[r0:r0 + t, g * gl:(g + 1) * gl]
            y = y * (zg * jax.nn.sigmoid(zg))
            o_ref[r0:r0 + t, g * gl:(g + 1) * gl] = _rms(y, ng_ref[:, g * gl:(g + 1) * gl]).astype(o_ref.dtype)


def _mamba2_ssd(z, xbc, dt, conv_w, conv_b, dt_bias, a_log, d_skip, norm_gain, chunks=4):
    bsz, s, w = z.shape
    ts = chunks * SSM_CHUNK
    row = lambda width: pl.BlockSpec((None, ts, width), lambda b, i: (b, i, 0))
    return pl.pallas_call(
        _ssd_kernel, grid=(bsz, s // ts),
        in_specs=[row(w), row(SSM_CONV_DIM), row(LANE), _full((CONV_WIDTH, SSM_CONV_DIM)),
                  _full((1, SSM_CONV_DIM)), _full((1, LANE)), _full((1, LANE)),
                  _full((1, w)), _full((1, w))],
        out_specs=row(w), out_shape=jax.ShapeDtypeStruct((bsz, s, w), BF16),
        scratch_shapes=[pltpu.VMEM((ts + CONV_PAD, SSM_CONV_DIM), F32),
                        pltpu.VMEM((SSM_GROUPS, SSM_STATE, GROUP_WIDTH // SSM_GROUPS), F32)],
        compiler_params=_params(2), name="ssd",
    )(z, xbc, dt, conv_w, conv_b, dt_bias, a_log, d_skip, norm_gain)


def _diff_kernel(tab_ref, qt_ref, k_ref, vt_ref, lq1, lk1, lq2, lk2, gain_ref, o_ref,
                 bias_scr, qm_scr, s_a, s_b, mc_a, mc_b, m_scr, acc_scr, ot_scr, *, lam_init):
    tq = DIFF_TQ
    qi = pl.program_id(1)
    n_sm = 2 * GROUP_HEADS
    grp = tq // SUBLANE
    acc_grp = DIFF_ACC_ROWS // SUBLANE
    s_bufs, mc_bufs = (s_a, s_b), (mc_a, mc_b)
    nbuf = len(s_bufs)

    @pl.when((pl.program_id(0) == 0) & (qi == 0))
    def _():
        ki_ = lax.broadcasted_iota(jnp.int32, (tq, tq), 0)
        qi_ = lax.broadcasted_iota(jnp.int32, (tq, tq), 1)
        for h in range(GROUP_HEADS):
            for d in range(DIFF_NEAR):
                dist = d * tq + qi_ - ki_
                bias = _bias_from_dist(jnp.maximum(dist, 0), tab_ref, GROUP_HEADS + h) * LOG2E
                bias_scr[d, h] = jnp.where(dist >= 0, bias, -jnp.inf)
            bias_scr[DIFF_NEAR, h] = jnp.full((tq, tq), tab_ref[NUM_BUCKETS - 1, GROUP_HEADS + h], F32) * LOG2E
            bias_scr[DIFF_NEAR + 1, h] = jnp.full((tq, tq), -jnp.inf, F32)

    qt = qt_ref[...]
    feat = lax.broadcasted_iota(jnp.int32, (GROUP_WIDTH, 1), 0)
    for idx in range(n_sm):
        qm_scr[idx] = jnp.where((feat // DIFF_QK_DIM) == idx, qt, jnp.zeros_like(qt))
    m_scr[...] = jnp.full(m_scr.shape, -1e30, F32)
    acc_scr[...] = jnp.zeros_like(acc_scr)

    c_far = [jnp.full((SUBLANE, tq), tab_ref[NUM_BUCKETS - 1, GROUP_HEADS + h], F32) * LOG2E
             for h in range(GROUP_HEADS)]
    ones_rows = jnp.ones((DIFF_ACC_ROWS - HEAD_DIM, tq), BF16)
    n_far = jnp.maximum(qi + 1 - DIFF_NEAR, 0)
    j_far = lax.div(jnp.maximum(n_far - 1, 0), 4 * nbuf)
    n_raw = 4 * nbuf * j_far

    def rows_max(x3):
        part = jnp.max(x3, axis=0)
        return jnp.broadcast_to(jnp.max(part, axis=0, keepdims=True), part.shape)

    def rows_sum(x3):
        part = jnp.sum(x3, axis=0)
        return jnp.broadcast_to(jnp.sum(part, axis=0, keepdims=True), part.shape)

    def key_block(ki):
        return k_ref[pl.ds(pl.multiple_of(jnp.minimum(ki, qi) * tq, tq), tq), :]

    def bias_tile(ki):
        d = qi - ki
        return jnp.where(d < 0, DIFF_NEAR + 1, jnp.minimum(d, DIFF_NEAR))

    def score_item(kb, idx, s_buf):
        s_buf[idx] = jnp.dot(kb, qm_scr[idx], preferred_element_type=F32)

    def max_any(tile, idx, s_buf, mc):
        t = s_buf[idx] + bias_scr[tile, idx // 2]
        s_buf[idx] = t
        mc[idx] = rows_max(t.reshape(grp, SUBLANE, tq))

    def max_far(tile, idx, s_buf, mc):
        mc[idx] = rows_max(s_buf[idx].reshape(grp, SUBLANE, tq)) + c_far[idx // 2]

    def exp_item(raw, vt, idx, s_buf, mc):
        h = idx // 2
        m_prev = m_scr[idx]
        m_next = jnp.maximum(m_prev, mc[idx])
        shift = m_next - jnp.where(raw, c_far[h], 0.0)
        p3 = jnp.exp2(s_buf[idx].reshape(grp, SUBLANE, tq) - shift[None])
        alpha = jnp.exp2(m_prev - m_next)
        lhs = jnp.concatenate([vt[h * HEAD_DIM:(h + 1) * HEAD_DIM, :], ones_rows], axis=0)
        pv = jnp.dot(lhs, p3.reshape(tq, tq).astype(BF16), preferred_element_type=F32)
        acc = acc_scr[idx].reshape(acc_grp, SUBLANE, tq) * alpha[None]
        acc_scr[idx] = acc.reshape(DIFF_ACC_ROWS, tq) + pv
        m_scr[idx] = m_next

    def trip(_, b0, max_item, nsub):
        for u in range(nsub):
            b = b0 + u
            cur, nxt = u % nbuf, (u + 1) % nbuf
            kb = key_block(b + 2)
            vt = vt_ref[jnp.minimum(b, qi)]
            raw = (b >= 1) & (b <= n_raw)
            tile = bias_tile(b + 1)
            for idx in range(n_sm):
                exp_item(raw, vt, idx, s_bufs[cur], mc_bufs[cur])
                score_item(kb, idx, s_bufs[cur])
                max_item(tile, idx, s_bufs[nxt], mc_bufs[nxt])
        return b0 + nsub

    for idx in range(n_sm):
        score_item(key_block(0), idx, s_bufs[0])
        score_item(key_block(1), idx, s_bufs[1])
    for idx in range(n_sm):
        max_any(bias_tile(0), idx, s_bufs[0], mc_bufs[0])
    long, short = 4 * nbuf, nbuf
    b0 = lax.fori_loop(0, j_far, functools.partial(trip, max_item=max_far, nsub=long), 0)
    left = qi + 1 - b0
    n_long = lax.div(left, long)
    b0 = lax.fori_loop(0, n_long, functools.partial(trip, max_item=max_any, nsub=long), b0)
    n_short = lax.div(left - n_long * long + short - 1, short)
    lax.fori_loop(0, n_short, functools.partial(trip, max_item=max_any, nsub=short), b0)

    lam = (jnp.exp(jnp.sum(lq1[...] * lk1[...], axis=1, keepdims=True))
           - jnp.exp(jnp.sum(lq2[...] * lk2[...], axis=1, keepdims=True)) + lam_init)
    for h in range(GROUP_HEADS):
        shape3 = (HEAD_DIM // SUBLANE, SUBLANE, tq)
        a1, a2 = acc_scr[2 * h], acc_scr[2 * h + 1]
        o1 = a1[0:HEAD_DIM].reshape(shape3) / a1[HEAD_DIM:HEAD_DIM + SUBLANE][None]
        o2 = a2[0:HEAD_DIM].reshape(shape3) / a2[HEAD_DIM:HEAD_DIM + SUBLANE][None]
        oh = o1 - lam * o2
        ms = rows_sum(oh * oh) * (1.0 / HEAD_DIM)
        oh = oh * lax.rsqrt(ms + NORM_EPS)[None]
        ot_scr[h * HEAD_DIM:(h + 1) * HEAD_DIM, :] = oh.reshape(HEAD_DIM, tq)
    o_ref[...] = ((ot_scr[...].T * gain_ref[...]) * (1.0 - lam_init)).astype(o_ref.dtype)


def _diff_attention(qt, k, vt, rel_bias, lq1, lk1, lq2, lk2, gain, lam_init):
    bsz, s, gw = k.shape
    tq = DIFF_TQ
    n_sm = 2 * GROUP_HEADS
    lvec = _full((1, DIFF_QK_DIM))
    stat = pltpu.VMEM((n_sm, SUBLANE, tq), F32)
    return pl.pallas_call(
        functools.partial(_diff_kernel, lam_init=lam_init), grid=(bsz, s // tq),
        in_specs=[pl.BlockSpec(memory_space=pltpu.SMEM),
                  pl.BlockSpec((None, None, gw, tq), lambda b, i: (b, i, 0, 0)),
                  pl.BlockSpec((None, s, gw), lambda b, i: (b, 0, 0)),
                  pl.BlockSpec((None, s // tq, gw, tq), lambda b, i: (b, 0, 0, 0)),
                  lvec, lvec, lvec, lvec, _full((1, gw))],
        out_specs=pl.BlockSpec((None, tq, gw), lambda b, i: (b, i, 0)),
        out_shape=jax.ShapeDtypeStruct((bsz, s, gw), BF16),
        scratch_shapes=[pltpu.VMEM((DIFF_NEAR + 2, GROUP_HEADS, tq, tq), F32),
                        pltpu.VMEM((n_sm, gw, tq), BF16),
                        pltpu.VMEM((n_sm, tq, tq), F32), pltpu.VMEM((n_sm, tq, tq), F32),
                        stat, stat, stat,
                        pltpu.VMEM((n_sm, DIFF_ACC_ROWS, tq), F32),
                        pltpu.VMEM((gw, tq), F32)],
        compiler_params=_params(2), name="diff_attn",
    )(rel_bias, qt, k, vt, lq1, lk1, lq2, lk2, gain)


def _mix_ffn_kernel(h_ref, ya, yb, yc, yd, wo_ref, g_mix, g_pre, wu_ref, wd_ref, g_post, o_ref, *, chunk):
    gw = GROUP_WIDTH
    acc = jnp.dot(ya[...], wo_ref[0:gw, :], preferred_element_type=F32)
    for j, y in enumerate((yb, yc, yd), start=1):
        acc = acc + jnp.dot(y[...], wo_ref[j * gw:(j + 1) * gw, :], preferred_element_type=F32)
    x = h_ref[...] + _rms(acc, g_mix[...])
    u = _rms(x, g_pre[...]).astype(BF16)
    acc = jnp.zeros(x.shape, F32)
    for c in range(wu_ref.shape[1] // chunk):
        f = jnp.dot(u, wu_ref[:, c * chunk:(c + 1) * chunk], preferred_element_type=F32)
        f = jnp.square(jnp.maximum(f, 0.0)).astype(BF16)
        acc = acc + jnp.dot(f, wd_ref[c * chunk:(c + 1) * chunk, :], preferred_element_type=F32)
    o_ref[...] = x + _rms(acc, g_post[...])


def _mix_ffn(h, ya, yb, yc, yd, w_out, g_mix, g_pre, w_up, w_down, g_post, tm=512, chunk=1024):
    bsz, s, d = h.shape
    row = lambda width: pl.BlockSpec((None, tm, width), lambda b, i: (b, i, 0))
    vec = _full((1, d))
    return pl.pallas_call(
        functools.partial(_mix_ffn_kernel, chunk=chunk), grid=(bsz, s // tm),
        in_specs=[row(d)] + [row(GROUP_WIDTH)] * 4 + [_resident(w_out.shape), vec, vec,
                                                      _resident(w_up.shape), _resident(w_down.shape), vec],
        out_specs=row(d), out_shape=jax.ShapeDtypeStruct(h.shape, F32), compiler_params=_params(2),
        name="mix_ffn",
    )(h, ya, yb, yc, yd, w_out, g_mix, g_pre, w_up, w_down, g_post)


def _block_diag(w):
    nb, n, _ = w.shape
    eye = jnp.eye(nb, dtype=w.dtype)
    return (eye[:, None, :, None] * w[:, :, None, :]).reshape(nb * n, nb * n)


def _pad_lanes(v):
    return jnp.pad(v.astype(F32), (0, LANE - v.shape[0]))[None, :]


def kernel(x, rel_bias, norm_mix_pre, norm_mix_post, norm_ffn_pre, norm_ffn_post, w_in, w_out,
           lru_conv_w, lru_conv_b, lru_wa, lru_ba, lru_wx, lru_bx, lru_lambda,
           ssm_conv_w, ssm_conv_b, ssm_dt_bias, ssm_a_log, ssm_d, ssm_norm,
           diff_lq1, diff_lk1, diff_lq2, diff_lk2, diff_norm, w_ff_up, w_ff_down):
    depth = w_in.shape[0]
    vec = lambda p: p.astype(F32)[None, :]
    h = x
    for layer in range(depth):
        w_all = _pack_w_in(w_in, layer)
        qa, ka, va, gb, xb, zc, xbc, qdt, kd, vdt, dt = _in_proj(h, vec(norm_mix_pre[layer]), w_all)

        ya = _dilated_attention(qa, ka, va, rel_bias)
        yb = _rg_lru(gb, xb, lru_conv_w[layer], vec(lru_conv_b[layer]),
                     _block_diag(lru_wa[layer]).astype(BF16), vec(lru_ba[layer]),
                     _block_diag(lru_wx[layer]).astype(BF16), vec(lru_bx[layer]), vec(lru_lambda[layer]))
        yc = _mamba2_ssd(zc, xbc, dt, ssm_conv_w[layer], vec(ssm_conv_b[layer]),
                         _pad_lanes(ssm_dt_bias[layer]), _pad_lanes(ssm_a_log[layer]),
                         jnp.repeat(ssm_d[layer].astype(F32), HEAD_DIM)[None, :], vec(ssm_norm[layer]))
        lam_init = 0.8 - 0.6 * math.exp(-0.3 * layer)
        yd = _diff_attention(qdt, kd, vdt, rel_bias, vec(diff_lq1[layer]), vec(diff_lk1[layer]),
                             vec(diff_lq2[layer]), vec(diff_lk2[layer]),
                             jnp.tile(diff_norm[layer].astype(F32), GROUP_HEADS)[None, :], lam_init)

        h = _mix_ffn(h, ya, yb, yc, yd, w_out[layer].astype(BF16), vec(norm_mix_post[layer]),
                     vec(norm_ffn_pre[layer]), w_ff_up[layer].astype(BF16), w_ff_down[layer].astype(BF16),
                     vec(norm_ffn_post[layer]))
    return h
```

```python
import functools
import math

import numpy as np
import jax
import jax.numpy as jnp
from jax import lax
from jax.experimental import pallas as pl
from jax.experimental.pallas import tpu as pltpu

F32 = jnp.float32
BF16 = jnp.bfloat16
HIGHEST = lax.Precision.HIGHEST

NORM_EPS = 1e-6
HEAD_DIM = 64
GROUP_HEADS = 4
GROUP_WIDTH = GROUP_HEADS * HEAD_DIM
NUM_BUCKETS = 32
MAX_DISTANCE = 2048
DILATED_PATTERNS = ((128, 1), (512, 4), (2048, 16))
DIL_BLOCK = 128
DIL_TILE = DIL_BLOCK * max(d for _, d in DILATED_PATTERNS)
LRU_C = 8.0
CONV_WIDTH = 4
SSM_GROUPS = 2
SSM_STATE = 128
SSM_CHUNK = 128
SSM_CONV_DIM = GROUP_WIDTH + 2 * SSM_GROUPS * SSM_STATE
DIFF_QK_DIM = HEAD_DIM // 2
DIFF_TQ = 256
DIFF_ACC_ROWS = HEAD_DIM + 16
LOG2E = math.log2(math.e)
LANE = 128
SUBLANE = 8
CONV_PAD = 8
VMEM_LIMIT = 48 * 1024 * 1024

_C_QA, _C_KA, _C_VA = 0, 256, 512
_C_GB, _C_XB = 768, 1024
_C_ZC, _C_XBC = 1280, 1536
_C_QD, _C_KD, _C_VD = 2304, 2560, 2816
_C_DT = 3072
_C_END = 3200
_W_IN_DT = 3 * GROUP_WIDTH + 2 * GROUP_WIDTH + GROUP_WIDTH + SSM_CONV_DIM
SSM_HEADS = GROUP_HEADS


def _t5_thresholds():
    n = np.arange(1, 4 * MAX_DISTANCE)
    max_exact = NUM_BUCKETS // 2
    large = max_exact + (np.log(n / max_exact) / math.log(MAX_DISTANCE / max_exact)
                         * (NUM_BUCKETS - max_exact)).astype(np.int64)
    bucket = np.where(n < max_exact, n, np.minimum(large, NUM_BUCKETS - 1))
    return tuple(int(n[bucket >= b].min()) for b in range(1, NUM_BUCKETS))


_T5_THR = _t5_thresholds()
DIFF_NEAR = -(-(_T5_THR[-1] + DIFF_TQ - 1) // DIFF_TQ)


def _bias_from_dist(dist, tab_ref, col):
    out = jnp.full(dist.shape, tab_ref[0, col], F32)
    for b in range(1, NUM_BUCKETS):
        out = jnp.where(dist >= _T5_THR[b - 1], tab_ref[b, col], out)
    return out


def _rms(x, g):
    return x * lax.rsqrt(jnp.mean(x * x, axis=-1, keepdims=True) + NORM_EPS) * g


def _params(n_axes):
    return pltpu.CompilerParams(dimension_semantics=("arbitrary",) * n_axes,
                                vmem_limit_bytes=VMEM_LIMIT)


def _full(shape):
    return pl.BlockSpec(shape, lambda *_: (0,) * len(shape))


def _resident(shape):
    return pl.BlockSpec(shape, lambda *_: (0,) * len(shape), pipeline_mode=pl.Buffered(1))


def _pack_w_in_kernel(w_ref, o_ref):
    d_cols = _C_DT - _W_IN_DT
    o_ref[:, 0:_W_IN_DT] = w_ref[:, 0:_W_IN_DT].astype(BF16)
    o_ref[:, _W_IN_DT:_C_DT] = w_ref[:, _W_IN_DT + SSM_HEADS:_W_IN_DT + SSM_HEADS + d_cols].astype(BF16)
    lane = lax.broadcasted_iota(jnp.int32, (1, LANE), 1)
    dt_tile = w_ref[:, _W_IN_DT:_W_IN_DT + LANE]
    o_ref[:, _C_DT:_C_END] = jnp.where(lane < SSM_HEADS, dt_tile, 0.0).astype(BF16)


def _pack_w_in(w_in, layer, tr=256):
    _, d, p_in = w_in.shape
    assert p_in == _C_DT + SSM_HEADS
    return pl.pallas_call(
        _pack_w_in_kernel, grid=(d // tr,),
        in_specs=[pl.BlockSpec((None, tr, p_in), lambda i: (layer, i, 0))],
        out_specs=pl.BlockSpec((tr, _C_END), lambda i: (i, 0)),
        out_shape=jax.ShapeDtypeStruct((d, _C_END), BF16), compiler_params=_params(1), name="pack_w_in",
    )(w_in)


def _in_proj_kernel(h_ref, g_ref, w_ref, qa, ka, va, gb, xb, zc, xbc, qdt, kd, vdt, dt):
    u = _rms(h_ref[...], g_ref[...]).astype(BF16)

    def seg(lo, hi):
        return jnp.dot(u, w_ref[:, lo:hi], preferred_element_type=F32)

    def store_transposed(ref, val):
        for j in range(ref.shape[0]):
            ref[j] = val[j * DIFF_TQ:(j + 1) * DIFF_TQ, :].T.astype(BF16)

    def store_halves(ref, val):
        for hh in range(ref.shape[0]):
            ref[hh] = val[:, hh * LANE:(hh + 1) * LANE]

    store_halves(qa, seg(_C_QA, _C_KA) * (HEAD_DIM ** -0.5))
    store_halves(ka, seg(_C_KA, _C_VA))
    store_halves(va, seg(_C_VA, _C_GB))
    gb[...] = seg(_C_GB, _C_XB)
    xb[...] = seg(_C_XB, _C_ZC)
    zc[...] = seg(_C_ZC, _C_XBC)
    xbc[...] = seg(_C_XBC, _C_QD)
    store_transposed(qdt, seg(_C_QD, _C_KD) * (DIFF_QK_DIM ** -0.5 * LOG2E))
    kd[...] = seg(_C_KD, _C_VD).astype(BF16)
    store_transposed(vdt, seg(_C_VD, _C_DT))
    dt[...] = seg(_C_DT, _C_END)


def _in_proj(h, gain, w_all, tm=512):
    bsz, s, d = h.shape
    gw = GROUP_WIDTH
    row = lambda width: pl.BlockSpec((None, tm, width), lambda b, i: (b, i, 0))
    shp = lambda width, dt: jax.ShapeDtypeStruct((bsz, s, width), dt)
    tshape = jax.ShapeDtypeStruct((bsz, s // DIFF_TQ, gw, DIFF_TQ), BF16)
    tspec = pl.BlockSpec((None, tm // DIFF_TQ, gw, DIFF_TQ), lambda b, i: (b, i, 0, 0))
    hshape = jax.ShapeDtypeStruct((bsz, gw // LANE, s, LANE), F32)
    hspec = pl.BlockSpec((None, gw // LANE, tm, LANE), lambda b, i: (b, 0, i, 0))
    out_shape = (hshape, hshape, hshape, shp(gw, F32), shp(gw, F32),
                 shp(gw, F32), shp(SSM_CONV_DIM, F32), tshape, shp(gw, BF16), tshape, shp(LANE, F32))
    out_specs = (hspec, hspec, hspec, row(gw), row(gw), row(gw), row(SSM_CONV_DIM),
                 tspec, row(gw), tspec, row(LANE))
    return pl.pallas_call(
        _in_proj_kernel, grid=(bsz, s // tm),
        in_specs=[row(d), _full((1, d)), _resident(w_all.shape)],
        out_specs=out_specs, out_shape=out_shape, compiler_params=_params(2), name="in_proj",
    )(h, gain, w_all)


def _dil_kernel(tab_ref, q_ref, k_ref, v_ref, out_ref, bias_scr, kbuf, vbuf, o_scr, lse_scr):
    tile = pl.program_id(1)
    tt = DIL_TILE
    nq, nk = DIL_BLOCK, 2 * DIL_BLOCK
    halves = GROUP_WIDTH // LANE
    heads_per_half = LANE // HEAD_DIM

    @pl.when((pl.program_id(0) == 0) & (tile == 0))
    def _():
        qi = lax.broadcasted_iota(jnp.int32, (nq, nk), 0)
        kj = lax.broadcasted_iota(jnp.int32, (nq, nk), 1)
        rel = qi + DIL_BLOCK - kj
        for p, (window, dil) in enumerate(DILATED_PATTERNS):
            valid = (rel >= 0) & (rel <= window // dil)
            dist = jnp.maximum(rel, 0) * dil
            for h in range(GROUP_HEADS):
                bias_scr[p, h] = jnp.where(valid, _bias_from_dist(dist, tab_ref, h), -jnp.inf)

    @pl.when(tile == 0)
    def _():
        kbuf[:, 0:tt, :] = jnp.zeros((halves, tt, LANE), F32)
        vbuf[:, 0:tt, :] = jnp.zeros((halves, tt, LANE), F32)

    kbuf[:, tt:2 * tt, :] = k_ref[...]
    vbuf[:, tt:2 * tt, :] = v_ref[...]

    head_in_half = lax.broadcasted_iota(jnp.int32, (1, LANE), 1) // HEAD_DIM
    kj = lax.broadcasted_iota(jnp.int32, (1, nk), 1)

    for p, (_, dil) in enumerate(DILATED_PATTERNS):
        nbt = tt // (DIL_BLOCK * dil)

        def body(j, carry, p=p, dil=dil, nbt=nbt):
            r, n = j // nbt, j % nbt
            start = n * (DIL_BLOCK * dil) + r
            kstart = start + tt - DIL_BLOCK * dil
            if dil == 1:
                rows = pl.ds(pl.multiple_of(start, DIL_BLOCK), nq)
                krows = pl.ds(pl.multiple_of(kstart, DIL_BLOCK), nk)
            else:
                rows = pl.ds(start, nq, stride=dil)
                krows = pl.ds(kstart, nk, stride=dil)
            key_ok = kj >= jnp.where(tile * nbt + n > 0, 0, DIL_BLOCK)
            scores = []
            for hh in range(halves):
                q = q_ref[hh, rows, :].astype(BF16)
                k = kbuf[hh, krows, :].astype(BF16)
                for hr in range(heads_per_half):
                    qh = jnp.where(head_in_half == hr, q, jnp.zeros_like(q))
                    scores.append(lax.dot_general(qh, k, (((1,), (1,)), ((), ())), preferred_element_type=F32))
            for hh in range(halves):
                v = vbuf[hh, krows, :].astype(BF16)
                o_acc = jnp.zeros((nq, LANE), F32)
                lse_acc = jnp.zeros((nq, LANE), F32)
                for hr in range(heads_per_half):
                    hm = head_in_half == hr
                    h = hh * heads_per_half + hr
                    sc = jnp.where(key_ok, scores[h] + bias_scr[p, h], -jnp.inf)
                    m = jnp.max(sc, axis=-1, keepdims=True)
                    e = jnp.exp(sc - m)
                    den = jnp.sum(e, axis=-1, keepdims=True)
                    oh = jnp.dot(e.astype(BF16), v, preferred_element_type=F32) / den
                    o_acc = jnp.where(hm, oh, o_acc)
                    lse_acc = jnp.where(hm, m + jnp.log(den), lse_acc)
                o_scr[p, hh, rows, :] = o_acc
                lse_scr[p, hh, rows, :] = lse_acc
            return carry

        lax.fori_loop(0, tt // DIL_BLOCK, body, 0, unroll=8)

    cm = 256

    def combine(c, carry):
        rows = pl.ds(pl.multiple_of(c * cm, cm), cm)
        for hh in range(halves):
            l0, l1, l2 = lse_scr[0, hh, rows, :], lse_scr[1, hh, rows, :], lse_scr[2, hh, rows, :]
            m = jnp.maximum(jnp.maximum(l0, l1), l2)
            w0, w1, w2 = jnp.exp(l0 - m), jnp.exp(l1 - m), jnp.exp(l2 - m)
            num = w0 * o_scr[0, hh, rows, :] + w1 * o_scr[1, hh, rows, :] + w2 * o_scr[2, hh, rows, :]
            out_ref[rows, hh * LANE:(hh + 1) * LANE] = (num / (w0 + w1 + w2)).astype(out_ref.dtype)
        return carry

    lax.fori_loop(0, tt // cm, combine, 0)
    kbuf[:, 0:tt, :] = k_ref[...]
    vbuf[:, 0:tt, :] = v_ref[...]


def _dilated_attention(q, k, v, rel_bias):
    bsz, halves, s, _ = q.shape
    tt = DIL_TILE
    n_pat = len(DILATED_PATTERNS)
    assert s % tt == 0 and n_pat == 3
    hspec = pl.BlockSpec((None, halves, tt, LANE), lambda b, i: (b, 0, i, 0))
    return pl.pallas_call(
        _dil_kernel, grid=(bsz, s // tt),
        in_specs=[pl.BlockSpec(memory_space=pltpu.SMEM), hspec, hspec, hspec],
        out_specs=pl.BlockSpec((None, tt, GROUP_WIDTH), lambda b, i: (b, i, 0)),
        out_shape=jax.ShapeDtypeStruct((bsz, s, GROUP_WIDTH), BF16),
        scratch_shapes=[pltpu.VMEM((n_pat, GROUP_HEADS, DIL_BLOCK, 2 * DIL_BLOCK), F32),
                        pltpu.VMEM((halves, 2 * tt, LANE), F32), pltpu.VMEM((halves, 2 * tt, LANE), F32),
                        pltpu.VMEM((n_pat, halves, tt, LANE), F32),
                        pltpu.VMEM((n_pat, halves, tt, LANE), F32)],
        compiler_params=_params(2), name="dilated_attn",
    )(rel_bias, q, k, v)


def _causal_conv(x, xbuf, cw_ref, cb_ref, first_tile):
    t = x.shape[0]

    @pl.when(first_tile)
    def _():
        xbuf[0:CONV_PAD, :] = jnp.zeros((CONV_PAD, x.shape[1]), F32)

    xbuf[CONV_PAD:CONV_PAD + t, :] = x
    y = cb_ref[...] + cw_ref[CONV_WIDTH - 1:CONV_WIDTH, :] * x
    for kk in range(CONV_WIDTH - 1):
        off = CONV_PAD - (CONV_WIDTH - 1) + kk
        y = y + cw_ref[kk:kk + 1, :] * xbuf[off:off + t, :]
    xbuf[0:CONV_PAD, :] = x[t - CONV_PAD:t, :]
    return y


def _lru_kernel(g_ref, x_ref, cw_ref, cb_ref, wa_ref, ba_ref, wx_ref, bx_ref, lam_ref, o_ref,
                xbuf, a_scr, b_scr, h_scr, hcar):
    first_tile = pl.program_id(1) == 0
    ts = x_ref.shape[0]

    @pl.when(first_tile)
    def _():
        hcar[...] = jnp.zeros_like(hcar)

    xc = _causal_conv(x_ref[...], xbuf, cw_ref, cb_ref, first_tile)
    xcb = xc.astype(BF16)
    r = jax.nn.sigmoid(jnp.dot(xcb, wa_ref[...], preferred_element_type=F32) + ba_ref[...])
    i = jax.nn.sigmoid(jnp.dot(xcb, wx_ref[...], preferred_element_type=F32) + bx_ref[...])
    neg_lam = -lam_ref[...]
    softplus = jnp.maximum(neg_lam, 0.0) + jnp.log1p(jnp.exp(-jnp.abs(neg_lam)))
    log_a = -LRU_C * r * softplus
    a = jnp.exp(log_a)
    a_scr[...] = a
    b_scr[...] = jnp.sqrt(-jnp.tanh(log_a) * (a * a + 1.0)) * (i * xc)

    row = lax.broadcasted_iota(jnp.int32, (SUBLANE, GROUP_WIDTH), 0)

    def body(j, hprev):
        r0 = pl.multiple_of(j * SUBLANE, SUBLANE)
        a = a_scr[pl.ds(r0, SUBLANE), :]
        b = b_scr[pl.ds(r0, SUBLANE), :]
        for d in (1, 2, 4):
            a_sh = jnp.where(row >= d, pltpu.roll(a, d, 0), 1.0)
            b_sh = jnp.where(row >= d, pltpu.roll(b, d, 0), 0.0)
            b = a * b_sh + b
            a = a * a_sh
        h_scr[pl.ds(r0, SUBLANE), :] = a * hprev + b
        a_last = jnp.broadcast_to(a[SUBLANE - 1:SUBLANE, :], a.shape)
        b_last = jnp.broadcast_to(b[SUBLANE - 1:SUBLANE, :], b.shape)
        return a_last * hprev + b_last

    hcar[...] = lax.fori_loop(0, ts // SUBLANE, body, hcar[...], unroll=8)
    o_ref[...] = (jax.nn.gelu(g_ref[...], approximate=True) * h_scr[...]).astype(o_ref.dtype)


def _rg_lru(gate, x, conv_w, conv_b, wa, ba, wx, bx, lam, ts=512):
    bsz, s, w = x.shape
    row = pl.BlockSpec((None, ts, w), lambda b, i: (b, i, 0))
    vec = _full((1, w))
    return pl.pallas_call(
        _lru_kernel, grid=(bsz, s // ts),
        in_specs=[row, row, _full((CONV_WIDTH, w)), vec, _full((w, w)), vec, _full((w, w)), vec, vec],
        out_specs=row, out_shape=jax.ShapeDtypeStruct((bsz, s, w), BF16),
        scratch_shapes=[pltpu.VMEM((ts + CONV_PAD, w), F32), pltpu.VMEM((ts, w), F32),
                        pltpu.VMEM((ts, w), F32), pltpu.VMEM((ts, w), F32), pltpu.VMEM((SUBLANE, w), F32)],
        compiler_params=_params(2), name="rg_lru",
    )(gate, x, conv_w, conv_b, wa, ba, wx, bx, lam)


def _ssd_kernel(z_ref, xbc_ref, dt_ref, cw_ref, cb_ref, dtb_ref, alog_ref, dsk_ref, ng_ref, o_ref,
                xbuf, st):
    first_tile = pl.program_id(1) == 0
    t = SSM_CHUNK
    gl = GROUP_WIDTH // SSM_GROUPS

    @pl.when(first_tile)
    def _():
        st[...] = jnp.zeros_like(st)

    xc_all = _causal_conv(xbc_ref[...], xbuf, cw_ref, cb_ref, first_tile)
    xc_all = xc_all * jax.nn.sigmoid(xc_all)
    dt_in = dt_ref[...] + dtb_ref[...]
    dtl_all = jnp.maximum(dt_in, 0.0) + jnp.log1p(jnp.exp(-jnp.abs(dt_in)))
    adt_all = dtl_all * (-jnp.exp(alog_ref[...]))
    li = lax.broadcasted_iota(jnp.int32, (t, t), 0)
    si = lax.broadcasted_iota(jnp.int32, (t, t), 1)
    causal = li >= si
    tri = causal.astype(F32)
    low = lax.broadcasted_iota(jnp.int32, (1, gl), 1) < HEAD_DIM

    def lanes(col):
        return jnp.broadcast_to(col, (t, LANE))

    for cc in range(xbc_ref.shape[0] // t):
        r0 = cc * t
        xc = xc_all[r0:r0 + t]
        dtl = dtl_all[r0:r0 + t]
        acum = jnp.dot(tri, adt_all[r0:r0 + t], precision=HIGHEST, preferred_element_type=F32)
        for g in range(SSM_GROUPS):
            xg = xc[:, g * gl:(g + 1) * gl]
            bg = xc[:, GROUP_WIDTH + g * SSM_STATE:GROUP_WIDTH + (g + 1) * SSM_STATE]
            cg = xc[:, GROUP_WIDTH + (SSM_GROUPS + g) * SSM_STATE:GROUP_WIDTH + (SSM_GROUPS + g + 1) * SSM_STATE]
            h0, h1 = 2 * g, 2 * g + 1
            xdt = (xg * jnp.where(low, lanes(dtl[:, h0:h0 + 1]), lanes(dtl[:, h1:h1 + 1]))).astype(BF16)
            cgb = cg.astype(BF16)
            cb = lax.dot_general(cgb, bg.astype(BF16), (((1,), (1,)), ((), ())), preferred_element_type=F32)
            state = st[g]
            y_off = jnp.dot(cgb, state.astype(BF16), preferred_element_type=F32)
            y_dg, st_new, ea, cdec = [], [], [], []
            for h in (h0, h1):
                ac = lanes(acum[:, h:h + 1])
                lmat = jnp.exp(jnp.where(causal, ac - ac.T, -jnp.inf))
                y_dg.append(jnp.dot((cb * lmat).astype(BF16), xdt, preferred_element_type=F32))
                a_last = ac[t - 1:t, :]
                bdec = bg * jnp.exp(a_last - ac)
                st_new.append(jnp.dot(bdec.T.astype(BF16), xdt, preferred_element_type=F32))
                ea.append(jnp.exp(ac))
                cdec.append(jnp.exp(a_last))
            y = (jnp.where(low, y_dg[0], y_dg[1]) + y_off * jnp.where(low, ea[0], ea[1])
                 + xg * dsk_ref[:, g * gl:(g + 1) * gl])
            st[g] = state * jnp.where(low, cdec[0], cdec[1]) + jnp.where(low, st_new[0], st_new[1])
            zg = z_ref[r0:r0 + t, g * gl:(g + 1) * gl]
            y = y * (zg * jax.nn.sigmoid(zg))
            o_ref[r0:r0 + t, g * gl:(g + 1) * gl] = _rms(y, ng_ref[:, g * gl:(g + 1) * gl]).astype(o_ref.dtype)


def _mamba2_ssd(z, xbc, dt, conv_w, conv_b, dt_bias, a_log, d_skip, norm_gain, chunks=4):
    bsz, s, w = z.shape
    ts = chunks * SSM_CHUNK
    row = lambda width: pl.BlockSpec((None, ts, width), lambda b, i: (b, i, 0))
    return pl.pallas_call(
        _ssd_kernel, grid=(bsz, s // ts),
        in_specs=[row(w), row(SSM_CONV_DIM), row(LANE), _full((CONV_WIDTH, SSM_CONV_DIM)),
                  _full((1, SSM_CONV_DIM)), _full((1, LANE)), _full((1, LANE)),
                  _full((1, w)), _full((1, w))],
        out_specs=row(w), out_shape=jax.ShapeDtypeStruct((bsz, s, w), BF16),
        scratch_shapes=[pltpu.VMEM((ts + CONV_PAD, SSM_CONV_DIM), F32),
                        pltpu.VMEM((SSM_GROUPS, SSM_STATE, GROUP_WIDTH // SSM_GROUPS), F32)],
        compiler_params=_params(2), name="ssd",
    )(z, xbc, dt, conv_w, conv_b, dt_bias, a_log, d_skip, norm_gain)


def _diff_kernel(tab_ref, qt_ref, k_ref, vt_ref, lq1, lk1, lq2, lk2, gain_ref, o_ref,
                 bias_scr, qm_scr, s_a, s_b, mc_a, mc_b, m_scr, acc_scr, ot_scr, *, lam_init):
    tq = DIFF_TQ
    qi = pl.program_id(1)
    n_sm = 2 * GROUP_HEADS
    grp = tq // SUBLANE
    acc_grp = DIFF_ACC_ROWS // SUBLANE
    s_bufs, mc_bufs = (s_a, s_b), (mc_a, mc_b)
    nbuf = len(s_bufs)

    @pl.when((pl.program_id(0) == 0) & (qi == 0))
    def _():
        ki_ = lax.broadcasted_iota(jnp.int32, (tq, tq), 0)
        qi_ = lax.broadcasted_iota(jnp.int32, (tq, tq), 1)
        for h in range(GROUP_HEADS):
            for d in range(DIFF_NEAR):
                dist = d * tq + qi_ - ki_
                bias = _bias_from_dist(jnp.maximum(dist, 0), tab_ref, GROUP_HEADS + h) * LOG2E
                bias_scr[d, h] = jnp.where(dist >= 0, bias, -jnp.inf)
            bias_scr[DIFF_NEAR, h] = jnp.full((tq, tq), tab_ref[NUM_BUCKETS - 1, GROUP_HEADS + h], F32) * LOG2E
            bias_scr[DIFF_NEAR + 1, h] = jnp.full((tq, tq), -jnp.inf, F32)

    qt = qt_ref[...]
    feat = lax.broadcasted_iota(jnp.int32, (GROUP_WIDTH, 1), 0)
    for idx in range(n_sm):
        qm_scr[idx] = jnp.where((feat // DIFF_QK_DIM) == idx, qt, jnp.zeros_like(qt))
    m_scr[...] = jnp.full(m_scr.shape, -1e30, F32)
    acc_scr[...] = jnp.zeros_like(acc_scr)

    c_far = [jnp.full((SUBLANE, tq), tab_ref[NUM_BUCKETS - 1, GROUP_HEADS + h], F32) * LOG2E
             for h in range(GROUP_HEADS)]
    ones_rows = jnp.ones((DIFF_ACC_ROWS - HEAD_DIM, tq), BF16)
    n_far = jnp.maximum(qi + 1 - DIFF_NEAR, 0)
    j_far = lax.div(jnp.maximum(n_far - 1, 0), 4 * nbuf)
    n_raw = 4 * nbuf * j_far

    def rows_max(x3):
        part = jnp.max(x3, axis=0)
        return jnp.broadcast_to(jnp.max(part, axis=0, keepdims=True), part.shape)

    def rows_sum(x3):
        part = jnp.sum(x3, axis=0)
        return jnp.broadcast_to(jnp.sum(part, axis=0, keepdims=True), part.shape)

    def key_block(ki):
        return k_ref[pl.ds(pl.multiple_of(jnp.minimum(ki, qi) * tq, tq), tq), :]

    def bias_tile(ki):
        d = qi - ki
        return jnp.where(d < 0, DIFF_NEAR + 1, jnp.minimum(d, DIFF_NEAR))

    def score_item(kb, idx, s_buf):
        s_buf[idx] = jnp.dot(kb, qm_scr[idx], preferred_element_type=F32)

    def max_any(tile, idx, s_buf, mc):
        t = s_buf[idx] + bias_scr[tile, idx // 2]
        s_buf[idx] = t
        mc[idx] = rows_max(t.reshape(grp, SUBLANE, tq))

    def max_far(tile, idx, s_buf, mc):
        mc[idx] = rows_max(s_buf[idx].reshape(grp, SUBLANE, tq)) + c_far[idx // 2]

    def exp_item(raw, vt, idx, s_buf, mc):
        h = idx // 2
        m_prev = m_scr[idx]
        m_next = jnp.maximum(m_prev, mc[idx])
        shift = m_next - jnp.where(raw, c_far[h], 0.0)
        p3 = jnp.exp2(s_buf[idx].reshape(grp, SUBLANE, tq) - shift[None])
        alpha = jnp.exp2(m_prev - m_next)
        lhs = jnp.concatenate([vt[h * HEAD_DIM:(h + 1) * HEAD_DIM, :], ones_rows], axis=0)
        pv = jnp.dot(lhs, p3.reshape(tq, tq).astype(BF16), preferred_element_type=F32)
        acc = acc_scr[idx].reshape(acc_grp, SUBLANE, tq) * alpha[None]
        acc_scr[idx] = acc.reshape(DIFF_ACC_ROWS, tq) + pv
        m_scr[idx] = m_next

    def trip(_, b0, max_item, nsub, refill=True):
        for u in range(nsub):
            b = b0 + u
            cur, nxt = u % nbuf, (u + 1) % nbuf
            kb = key_block(b + 2) if refill else None
            vt = vt_ref[jnp.minimum(b, qi)]
            raw = (b >= 1) & (b <= n_raw)
            tile = bias_tile(b + 1)
            for idx in range(n_sm):
                exp_item(raw, vt, idx, s_bufs[cur], mc_bufs[cur])
                if refill:
                    score_item(kb, idx, s_bufs[cur])
                if refill or u + 1 < nsub:
                    max_item(tile, idx, s_bufs[nxt], mc_bufs[nxt])
        return b0 + nsub

    for idx in range(n_sm):
        score_item(key_block(0), idx, s_bufs[0])
        score_item(key_block(1), idx, s_bufs[1])
    for idx in range(n_sm):
        max_any(bias_tile(0), idx, s_bufs[0], mc_bufs[0])
    long, short = 4 * nbuf, nbuf
    b0 = lax.fori_loop(0, j_far, functools.partial(trip, max_item=max_far, nsub=long), 0)
    n_even = short * lax.div(qi + short, short)
    left = n_even - short - b0
    n_long = lax.div(left, long)
    b0 = lax.fori_loop(0, n_long, functools.partial(trip, max_item=max_any, nsub=long), b0)
    n_short = lax.div(left - n_long * long, short)
    b0 = lax.fori_loop(0, n_short, functools.partial(trip, max_item=max_any, nsub=short), b0)
    trip(0, b0, max_item=max_any, nsub=short, refill=False)

    lam = (jnp.exp(jnp.sum(lq1[...] * lk1[...], axis=1, keepdims=True))
           - jnp.exp(jnp.sum(lq2[...] * lk2[...], axis=1, keepdims=True)) + lam_init)
    for h in range(GROUP_HEADS):
        shape3 = (HEAD_DIM // SUBLANE, SUBLANE, tq)
        a1, a2 = acc_scr[2 * h], acc_scr[2 * h + 1]
        o1 = a1[0:HEAD_DIM].reshape(shape3) / a1[HEAD_DIM:HEAD_DIM + SUBLANE][None]
        o2 = a2[0:HEAD_DIM].reshape(shape3) / a2[HEAD_DIM:HEAD_DIM + SUBLANE][None]
        oh = o1 - lam * o2
        ms = rows_sum(oh * oh) * (1.0 / HEAD_DIM)
        oh = oh * lax.rsqrt(ms + NORM_EPS)[None]
        ot_scr[h * HEAD_DIM:(h + 1) * HEAD_DIM, :] = oh.reshape(HEAD_DIM, tq)
    o_ref[...] = ((ot_scr[...].T * gain_ref[...]) * (1.0 - lam_init)).astype(o_ref.dtype)


def _diff_attention(qt, k, vt, rel_bias, lq1, lk1, lq2, lk2, gain, lam_init):
    bsz, s, gw = k.shape
    tq = DIFF_TQ
    n_sm = 2 * GROUP_HEADS
    lvec = _full((1, DIFF_QK_DIM))
    stat = pltpu.VMEM((n_sm, SUBLANE, tq), F32)
    return pl.pallas_call(
        functools.partial(_diff_kernel, lam_init=lam_init), grid=(bsz, s // tq),
        in_specs=[pl.BlockSpec(memory_space=pltpu.SMEM),
                  pl.BlockSpec((None, None, gw, tq), lambda b, i: (b, i, 0, 0)),
                  pl.BlockSpec((None, s, gw), lambda b, i: (b, 0, 0)),
                  pl.BlockSpec((None, s // tq, gw, tq), lambda b, i: (b, 0, 0, 0)),
                  lvec, lvec, lvec, lvec, _full((1, gw))],
        out_specs=pl.BlockSpec((None, tq, gw), lambda b, i: (b, i, 0)),
        out_shape=jax.ShapeDtypeStruct((bsz, s, gw), BF16),
        scratch_shapes=[pltpu.VMEM((DIFF_NEAR + 2, GROUP_HEADS, tq, tq), F32),
                        pltpu.VMEM((n_sm, gw, tq), BF16),
                        pltpu.VMEM((n_sm, tq, tq), F32), pltpu.VMEM((n_sm, tq, tq), F32),
                        stat, stat, stat,
                        pltpu.VMEM((n_sm, DIFF_ACC_ROWS, tq), F32),
                        pltpu.VMEM((gw, tq), F32)],
        compiler_params=_params(2), name="diff_attn",
    )(rel_bias, qt, k, vt, lq1, lk1, lq2, lk2, gain)


def _mix_ffn_kernel(h_ref, ya, yb, yc, yd, wo_ref, g_mix, g_pre, wu_ref, wd_ref, g_post, o_ref, *, chunk):
    gw = GROUP_WIDTH
    acc = jnp.dot(ya[...], wo_ref[0:gw, :], preferred_element_type=F32)
    for j, y in enumerate((yb, yc, yd), start=1):
        acc = acc + jnp.dot(y[...], wo_ref[j * gw:(j + 1) * gw, :], preferred_element_type=F32)
    x = h_ref[...] + _rms(acc, g_mix[...])
    u = _rms(x, g_pre[...]).astype(BF16)
    acc = jnp.zeros(x.shape, F32)
    for c in range(wu_ref.shape[1] // chunk):
        f = jnp.dot(u, wu_ref[:, c * chunk:(c + 1) * chunk], preferred_element_type=F32)
        f = jnp.square(jnp.maximum(f, 0.0)).astype(BF16)
        acc = acc + jnp.dot(f, wd_ref[c * chunk:(c + 1) * chunk, :], preferred_element_type=F32)
    o_ref[...] = x + _rms(acc, g_post[...])


def _mix_ffn(h, ya, yb, yc, yd, w_out, g_mix, g_pre, w_up, w_down, g_post, tm=512, chunk=1024):
    bsz, s, d = h.shape
    row = lambda width: pl.BlockSpec((None, tm, width), lambda b, i: (b, i, 0))
    vec = _full((1, d))
    return pl.pallas_call(
        functools.partial(_mix_ffn_kernel, chunk=chunk), grid=(bsz, s // tm),
        in_specs=[row(d)] + [row(GROUP_WIDTH)] * 4 + [_resident(w_out.shape), vec, vec,
                                                      _resident(w_up.shape), _resident(w_down.shape), vec],
        out_specs=row(d), out_shape=jax.ShapeDtypeStruct(h.shape, F32), compiler_params=_params(2),
        name="mix_ffn",
    )(h, ya, yb, yc, yd, w_out, g_mix, g_pre, w_up, w_down, g_post)


def _block_diag(w):
    nb, n, _ = w.shape
    eye = jnp.eye(nb, dtype=w.dtype)
    return (eye[:, None, :, None] * w[:, :, None, :]).reshape(nb * n, nb * n)


def _pad_lanes(v):
    return jnp.pad(v.astype(F32), (0, LANE - v.shape[0]))[None, :]


def kernel(x, rel_bias, norm_mix_pre, norm_mix_post, norm_ffn_pre, norm_ffn_post, w_in, w_out,
           lru_conv_w, lru_conv_b, lru_wa, lru_ba, lru_wx, lru_bx, lru_lambda,
           ssm_conv_w, ssm_conv_b, ssm_dt_bias, ssm_a_log, ssm_d, ssm_norm,
           diff_lq1, diff_lk1, diff_lq2, diff_lk2, diff_norm, w_ff_up, w_ff_down):
    depth = w_in.shape[0]
    vec = lambda p: p.astype(F32)[None, :]
    h = x
    for layer in range(depth):
        w_all = _pack_w_in(w_in, layer)
        qa, ka, va, gb, xb, zc, xbc, qdt, kd, vdt, dt = _in_proj(h, vec(norm_mix_pre[layer]), w_all)

        ya = _dilated_attention(qa, ka, va, rel_bias)
        yb = _rg_lru(gb, xb, lru_conv_w[layer], vec(lru_conv_b[layer]),
                     _block_diag(lru_wa[layer]).astype(BF16), vec(lru_ba[layer]),
                     _block_diag(lru_wx[layer]).astype(BF16), vec(lru_bx[layer]), vec(lru_lambda[layer]))
        yc = _mamba2_ssd(zc, xbc, dt, ssm_conv_w[layer], vec(ssm_conv_b[layer]),
                         _pad_lanes(ssm_dt_bias[layer]), _pad_lanes(ssm_a_log[layer]),
                         jnp.repeat(ssm_d[layer].astype(F32), HEAD_DIM)[None, :], vec(ssm_norm[layer]))
        lam_init = 0.8 - 0.6 * math.exp(-0.3 * layer)
        yd = _diff_attention(qdt, kd, vdt, rel_bias, vec(diff_lq1[layer]), vec(diff_lk1[layer]),
                             vec(diff_lq2[layer]), vec(diff_lk2[layer]),
                             jnp.tile(diff_norm[layer].astype(F32), GROUP_HEADS)[None, :], lam_init)

        h = _mix_ffn(h, ya, yb, yc, yd, w_out[layer].astype(BF16), vec(norm_mix_post[layer]),
                     vec(norm_ffn_pre[layer]), w_ff_up[layer].astype(BF16), w_ff_down[layer].astype(BF16),
                     vec(norm_ffn_post[layer]))
    return h
```

```python
import functools
import math

import numpy as np
import jax
import jax.numpy as jnp
from jax import lax
from jax.experimental import pallas as pl
from jax.experimental.pallas import tpu as pltpu

F32 = jnp.float32
BF16 = jnp.bfloat16
HIGHEST = lax.Precision.HIGHEST

NORM_EPS = 1e-6
HEAD_DIM = 64
GROUP_HEADS = 4
GROUP_WIDTH = GROUP_HEADS * HEAD_DIM
NUM_BUCKETS = 32
MAX_DISTANCE = 2048
DILATED_PATTERNS = ((128, 1), (512, 4), (2048, 16))
DIL_BLOCK = 128
DIL_TILE = DIL_BLOCK * max(d for _, d in DILATED_PATTERNS)
LRU_C = 8.0
CONV_WIDTH = 4
SSM_GROUPS = 2
SSM_STATE = 128
SSM_CHUNK = 128
SSM_CONV_DIM = GROUP_WIDTH + 2 * SSM_GROUPS * SSM_STATE
DIFF_QK_DIM = HEAD_DIM // 2
DIFF_TQ = 256
DIFF_ACC_ROWS = HEAD_DIM + 16
LOG2E = math.log2(math.e)
LN2 = math.log(2.0)
LANE = 128
SUBLANE = 8
CONV_PAD = 8
VMEM_LIMIT = 48 * 1024 * 1024

_C_QA, _C_KA, _C_VA = 0, 256, 512
_C_GB, _C_XB = 768, 1024
_C_ZC, _C_XBC = 1280, 1536
_C_QD, _C_KD, _C_VD = 2304, 2560, 2816
_C_DT = 3072
_C_END = 3200
_W_IN_DT = 3 * GROUP_WIDTH + 2 * GROUP_WIDTH + GROUP_WIDTH + SSM_CONV_DIM
SSM_HEADS = GROUP_HEADS


def _t5_thresholds():
    n = np.arange(1, 4 * MAX_DISTANCE)
    max_exact = NUM_BUCKETS // 2
    large = max_exact + (np.log(n / max_exact) / math.log(MAX_DISTANCE / max_exact)
                         * (NUM_BUCKETS - max_exact)).astype(np.int64)
    bucket = np.where(n < max_exact, n, np.minimum(large, NUM_BUCKETS - 1))
    return tuple(int(n[bucket >= b].min()) for b in range(1, NUM_BUCKETS))


_T5_THR = _t5_thresholds()
DIFF_NEAR = -(-(_T5_THR[-1] + DIFF_TQ - 1) // DIFF_TQ)


def _bias_from_dist(dist, tab_ref, col):
    out = jnp.full(dist.shape, tab_ref[0, col], F32)
    for b in range(1, NUM_BUCKETS):
        out = jnp.where(dist >= _T5_THR[b - 1], tab_ref[b, col], out)
    return out


def _rms(x, g):
    return x * lax.rsqrt(jnp.mean(x * x, axis=-1, keepdims=True) + NORM_EPS) * g


def _params(n_axes):
    return pltpu.CompilerParams(dimension_semantics=("arbitrary",) * n_axes,
                                vmem_limit_bytes=VMEM_LIMIT)


def _full(shape):
    return pl.BlockSpec(shape, lambda *_: (0,) * len(shape))


def _resident(shape):
    return pl.BlockSpec(shape, lambda *_: (0,) * len(shape), pipeline_mode=pl.Buffered(1))


def _pack_w_in_kernel(w_ref, o_ref):
    d_cols = _C_DT - _W_IN_DT
    o_ref[:, 0:_W_IN_DT] = w_ref[:, 0:_W_IN_DT].astype(BF16)
    o_ref[:, _W_IN_DT:_C_DT] = w_ref[:, _W_IN_DT + SSM_HEADS:_W_IN_DT + SSM_HEADS + d_cols].astype(BF16)
    lane = lax.broadcasted_iota(jnp.int32, (1, LANE), 1)
    dt_tile = w_ref[:, _W_IN_DT:_W_IN_DT + LANE]
    o_ref[:, _C_DT:_C_END] = jnp.where(lane < SSM_HEADS, dt_tile, 0.0).astype(BF16)


def _pack_w_in(w_in, layer, tr=256):
    depth, d, p_in = w_in.shape
    assert p_in == _C_DT + SSM_HEADS
    return pl.pallas_call(
        _pack_w_in_kernel, grid=(d // tr,),
        in_specs=[pl.BlockSpec((tr, p_in), lambda i: (layer * (d // tr) + i, 0))],
        out_specs=pl.BlockSpec((tr, _C_END), lambda i: (i, 0)),
        out_shape=jax.ShapeDtypeStruct((d, _C_END), BF16), compiler_params=_params(1), name="pack_w_in",
    )(w_in.reshape(depth * d, p_in))


def _in_proj_kernel(h_ref, g_ref, w_ref, qa, ka, va, gb, xb, zc, xbc, qdt, kd, vdt, dt):
    u = _rms(h_ref[...], g_ref[...]).astype(BF16)

    def seg(lo, hi):
        return jnp.dot(u, w_ref[:, lo:hi], preferred_element_type=F32)

    def store_transposed(ref, val):
        for j in range(ref.shape[0]):
            ref[j] = val[j * DIFF_TQ:(j + 1) * DIFF_TQ, :].T.astype(BF16)

    def store_halves(ref, val):
        for hh in range(ref.shape[0]):
            ref[hh] = val[:, hh * LANE:(hh + 1) * LANE]

    store_halves(qa, seg(_C_QA, _C_KA) * (HEAD_DIM ** -0.5 * LOG2E))
    store_halves(ka, seg(_C_KA, _C_VA))
    store_halves(va, seg(_C_VA, _C_GB))
    gb[...] = seg(_C_GB, _C_XB)
    xb[...] = seg(_C_XB, _C_ZC)
    zc[...] = seg(_C_ZC, _C_XBC)
    xbc[...] = seg(_C_XBC, _C_QD)
    store_transposed(qdt, seg(_C_QD, _C_KD) * (DIFF_QK_DIM ** -0.5 * LOG2E))
    kd[...] = seg(_C_KD, _C_VD).astype(BF16)
    store_transposed(vdt, seg(_C_VD, _C_DT))
    dt[...] = seg(_C_DT, _C_END)


def _in_proj(h, gain, w_all, tm=1024):
    bsz, s, d = h.shape
    gw = GROUP_WIDTH
    row = lambda width: pl.BlockSpec((None, tm, width), lambda b, i: (b, i, 0))
    shp = lambda width, dt: jax.ShapeDtypeStruct((bsz, s, width), dt)
    tshape = jax.ShapeDtypeStruct((bsz, s // DIFF_TQ, gw, DIFF_TQ), BF16)
    tspec = pl.BlockSpec((None, tm // DIFF_TQ, gw, DIFF_TQ), lambda b, i: (b, i, 0, 0))
    hshape = jax.ShapeDtypeStruct((bsz, gw // LANE, s, LANE), F32)
    hspec = pl.BlockSpec((None, gw // LANE, tm, LANE), lambda b, i: (b, 0, i, 0))
    out_shape = (hshape, hshape, hshape, shp(gw, F32), shp(gw, F32),
                 shp(gw, F32), shp(SSM_CONV_DIM, F32), tshape, shp(gw, BF16), tshape, shp(LANE, F32))
    out_specs = (hspec, hspec, hspec, row(gw), row(gw), row(gw), row(SSM_CONV_DIM),
                 tspec, row(gw), tspec, row(LANE))
    return pl.pallas_call(
        _in_proj_kernel, grid=(bsz, s // tm),
        in_specs=[row(d), _full((1, d)), _resident(w_all.shape)],
        out_specs=out_specs, out_shape=out_shape, compiler_params=_params(2), name="in_proj",
    )(h, gain, w_all)


def _dil_kernel(tab_ref, q_ref, k_ref, v_ref, out_ref, bias_scr, kbuf, vbuf, o_scr, lse_scr):
    tile = pl.program_id(1)
    tt = DIL_TILE
    nq, nk = DIL_BLOCK, 2 * DIL_BLOCK
    halves = GROUP_WIDTH // LANE
    heads_per_half = LANE // HEAD_DIM

    @pl.when((pl.program_id(0) == 0) & (tile == 0))
    def _():
        qi = lax.broadcasted_iota(jnp.int32, (nq, nk), 0)
        kj = lax.broadcasted_iota(jnp.int32, (nq, nk), 1)
        rel = qi + DIL_BLOCK - kj
        for p, (window, dil) in enumerate(DILATED_PATTERNS):
            valid = (rel >= 0) & (rel <= window // dil)
            dist = jnp.maximum(rel, 0) * dil
            for h in range(GROUP_HEADS):
                bias = _bias_from_dist(dist, tab_ref, h) * LOG2E
                bias_scr[0, p, h] = jnp.where(valid, bias, -jnp.inf)
                bias_scr[1, p, h] = jnp.where(valid & (kj >= DIL_BLOCK), bias, -jnp.inf)

    @pl.when(tile == 0)
    def _():
        kbuf[:, 0:tt, :] = jnp.zeros((halves, tt, LANE), F32)
        vbuf[:, 0:tt, :] = jnp.zeros((halves, tt, LANE), F32)

    kbuf[:, tt:2 * tt, :] = k_ref[...]
    vbuf[:, tt:2 * tt, :] = v_ref[...]

    head_in_half = lax.broadcasted_iota(jnp.int32, (1, LANE), 1) // HEAD_DIM

    for p, (_, dil) in enumerate(DILATED_PATTERNS):
        nbt = tt // (DIL_BLOCK * dil)

        def body(j, carry, p=p, dil=dil, nbt=nbt):
            r, n = j // nbt, j % nbt
            start = n * (DIL_BLOCK * dil) + r
            kstart = start + tt - DIL_BLOCK * dil
            if dil == 1:
                rows = pl.ds(pl.multiple_of(start, DIL_BLOCK), nq)
                krows = pl.ds(pl.multiple_of(kstart, DIL_BLOCK), nk)
            else:
                rows = pl.ds(start, nq, stride=dil)
                krows = pl.ds(kstart, nk, stride=dil)
            variant = jnp.where(tile * nbt + n > 0, 0, 1)
            scores = []
            for hh in range(halves):
                q = q_ref[hh, rows, :].astype(BF16)
                k = kbuf[hh, krows, :].astype(BF16)
                for hr in range(heads_per_half):
                    qh = jnp.where(head_in_half == hr, q, jnp.zeros_like(q))
                    scores.append(lax.dot_general(qh, k, (((1,), (1,)), ((), ())), preferred_element_type=F32))
            for hh in range(halves):
                v = vbuf[hh, krows, :].astype(BF16)
                o_acc = jnp.zeros((nq, LANE), F32)
                lse_acc = jnp.zeros((nq, LANE), F32)
                for hr in range(heads_per_half):
                    hm = head_in_half == hr
                    h = hh * heads_per_half + hr
                    sc = scores[h] + bias_scr[variant, p, h]
                    m = jnp.max(sc, axis=-1, keepdims=True)
                    e = jnp.exp2(sc - m)
                    den = jnp.sum(e, axis=-1, keepdims=True)
                    oh = jnp.dot(e.astype(BF16), v, preferred_element_type=F32) / den
                    o_acc = jnp.where(hm, oh, o_acc)
                    lse_acc = jnp.where(hm, m * LN2 + jnp.log(den), lse_acc)
                o_scr[p, hh, rows, :] = o_acc
                lse_scr[p, hh, rows, :] = lse_acc
            return carry

        lax.fori_loop(0, tt // DIL_BLOCK, body, 0, unroll=8)

    cm = 256

    def combine(c, carry):
        rows = pl.ds(pl.multiple_of(c * cm, cm), cm)
        for hh in range(halves):
            l0, l1, l2 = lse_scr[0, hh, rows, :], lse_scr[1, hh, rows, :], lse_scr[2, hh, rows, :]
            m = jnp.maximum(jnp.maximum(l0, l1), l2)
            w0, w1, w2 = jnp.exp(l0 - m), jnp.exp(l1 - m), jnp.exp(l2 - m)
            num = w0 * o_scr[0, hh, rows, :] + w1 * o_scr[1, hh, rows, :] + w2 * o_scr[2, hh, rows, :]
            out_ref[rows, hh * LANE:(hh + 1) * LANE] = (num / (w0 + w1 + w2)).astype(out_ref.dtype)
        return carry

    lax.fori_loop(0, tt // cm, combine, 0)
    kbuf[:, 0:tt, :] = k_ref[...]
    vbuf[:, 0:tt, :] = v_ref[...]


def _dilated_attention(q, k, v, rel_bias):
    bsz, halves, s, _ = q.shape
    tt = DIL_TILE
    n_pat = len(DILATED_PATTERNS)
    assert s % tt == 0 and n_pat == 3
    hspec = pl.BlockSpec((None, halves, tt, LANE), lambda b, i: (b, 0, i, 0))
    return pl.pallas_call(
        _dil_kernel, grid=(bsz, s // tt),
        in_specs=[pl.BlockSpec(memory_space=pltpu.SMEM), hspec, hspec, hspec],
        out_specs=pl.BlockSpec((None, tt, GROUP_WIDTH), lambda b, i: (b, i, 0)),
        out_shape=jax.ShapeDtypeStruct((bsz, s, GROUP_WIDTH), BF16),
        scratch_shapes=[pltpu.VMEM((2, n_pat, GROUP_HEADS, DIL_BLOCK, 2 * DIL_BLOCK), F32),
                        pltpu.VMEM((halves, 2 * tt, LANE), F32), pltpu.VMEM((halves, 2 * tt, LANE), F32),
                        pltpu.VMEM((n_pat, halves, tt, LANE), F32),
                        pltpu.VMEM((n_pat, halves, tt, LANE), F32)],
        compiler_params=_params(2), name="dilated_attn",
    )(rel_bias, q, k, v)


def _causal_conv(x, xbuf, cw_ref, cb_ref, first_tile):
    t = x.shape[0]

    @pl.when(first_tile)
    def _():
        xbuf[0:CONV_PAD, :] = jnp.zeros((CONV_PAD, x.shape[1]), F32)

    xbuf[CONV_PAD:CONV_PAD + t, :] = x
    y = cb_ref[...] + cw_ref[CONV_WIDTH - 1:CONV_WIDTH, :] * x
    for kk in range(CONV_WIDTH - 1):
        off = CONV_PAD - (CONV_WIDTH - 1) + kk
        y = y + cw_ref[kk:kk + 1, :] * xbuf[off:off + t, :]
    xbuf[0:CONV_PAD, :] = x[t - CONV_PAD:t, :]
    return y


def _lru_kernel(g_ref, x_ref, cw_ref, cb_ref, wa_ref, ba_ref, wx_ref, bx_ref, lam_ref, o_ref,
                xbuf, a_scr, b_scr, h_scr, hcar):
    first_tile = pl.program_id(1) == 0
    ts = x_ref.shape[0]

    @pl.when(first_tile)
    def _():
        hcar[...] = jnp.zeros_like(hcar)

    xc = _causal_conv(x_ref[...], xbuf, cw_ref, cb_ref, first_tile)
    xcb = xc.astype(BF16)
    r = jax.nn.sigmoid(jnp.dot(xcb, wa_ref[...], preferred_element_type=F32) + ba_ref[...])
    i = jax.nn.sigmoid(jnp.dot(xcb, wx_ref[...], preferred_element_type=F32) + bx_ref[...])
    neg_lam = -lam_ref[...]
    softplus = jnp.maximum(neg_lam, 0.0) + jnp.log1p(jnp.exp(-jnp.abs(neg_lam)))
    log_a = -LRU_C * r * softplus
    a = jnp.exp(log_a)
    a_scr[...] = a
    b_scr[...] = jnp.sqrt(-jnp.tanh(log_a) * (a * a + 1.0)) * (i * xc)

    row = lax.broadcasted_iota(jnp.int32, (SUBLANE, GROUP_WIDTH), 0)

    def body(j, hprev):
        r0 = pl.multiple_of(j * SUBLANE, SUBLANE)
        a = a_scr[pl.ds(r0, SUBLANE), :]
        b = b_scr[pl.ds(r0, SUBLANE), :]
        for d in (1, 2, 4):
            a_sh = jnp.where(row >= d, pltpu.roll(a, d, 0), 1.0)
            b_sh = jnp.where(row >= d, pltpu.roll(b, d, 0), 0.0)
            b = a * b_sh + b
            a = a * a_sh
        h_scr[pl.ds(r0, SUBLANE), :] = a * hprev + b
        a_last = jnp.broadcast_to(a[SUBLANE - 1:SUBLANE, :], a.shape)
        b_last = jnp.broadcast_to(b[SUBLANE - 1:SUBLANE, :], b.shape)
        return a_last * hprev + b_last

    hcar[...] = lax.fori_loop(0, ts // SUBLANE, body, hcar[...], unroll=8)
    o_ref[...] = (jax.nn.gelu(g_ref[...], approximate=True) * h_scr[...]).astype(o_ref.dtype)


def _rg_lru(gate, x, conv_w, conv_b, wa, ba, wx, bx, lam, ts=512):
    bsz, s, w = x.shape
    row = pl.BlockSpec((None, ts, w), lambda b, i: (b, i, 0))
    vec = _full((1, w))
    return pl.pallas_call(
        _lru_kernel, grid=(bsz, s // ts),
        in_specs=[row, row, _full((CONV_WIDTH, w)), vec, _full((w, w)), vec, _full((w, w)), vec, vec],
        out_specs=row, out_shape=jax.ShapeDtypeStruct((bsz, s, w), BF16),
        scratch_shapes=[pltpu.VMEM((ts + CONV_PAD, w), F32), pltpu.VMEM((ts, w), F32),
                        pltpu.VMEM((ts, w), F32), pltpu.VMEM((ts, w), F32), pltpu.VMEM((SUBLANE, w), F32)],
        compiler_params=_params(2), name="rg_lru",
    )(gate, x, conv_w, conv_b, wa, ba, wx, bx, lam)


def _ssd_kernel(z_ref, xbc_ref, dt_ref, cw_ref, cb_ref, dtb_ref, alog_ref, dsk_ref, ng_ref, o_ref,
                xbuf, st):
    first_tile = pl.program_id(1) == 0
    t = SSM_CHUNK
    gl = GROUP_WIDTH // SSM_GROUPS

    @pl.when(first_tile)
    def _():
        st[...] = jnp.zeros_like(st)

    xc_all = _causal_conv(xbc_ref[...], xbuf, cw_ref, cb_ref, first_tile)
    xc_all = xc_all * jax.nn.sigmoid(xc_all)
    dt_in = dt_ref[...] + dtb_ref[...]
    dtl_all = jnp.maximum(dt_in, 0.0) + jnp.log1p(jnp.exp(-jnp.abs(dt_in)))
    adt_all = dtl_all * (-jnp.exp(alog_ref[...]))
    li = lax.broadcasted_iota(jnp.int32, (t, t), 0)
    si = lax.broadcasted_iota(jnp.int32, (t, t), 1)
    causal = li >= si
    tri = causal.astype(F32)
    low = lax.broadcasted_iota(jnp.int32, (1, gl), 1) < HEAD_DIM

    def lanes(col):
        return jnp.broadcast_to(col, (t, LANE))

    for cc in range(xbc_ref.shape[0] // t):
        r0 = cc * t
        xc = xc_all[r0:r0 + t]
        dtl = dtl_all[r0:r0 + t]
        acum = jnp.dot(tri, adt_all[r0:r0 + t], precision=HIGHEST, preferred_element_type=F32)
        for g in range(SSM_GROUPS):
            xg = xc[:, g * gl:(g + 1) * gl]
            bg = xc[:, GROUP_WIDTH + g * SSM_STATE:GROUP_WIDTH + (g + 1) * SSM_STATE]
            cg = xc[:, GROUP_WIDTH + (SSM_GROUPS + g) * SSM_STATE:GROUP_WIDTH + (SSM_GROUPS + g + 1) * SSM_STATE]
            h0, h1 = 2 * g, 2 * g + 1
            xdt = (xg * jnp.where(low, lanes(dtl[:, h0:h0 + 1]), lanes(dtl[:, h1:h1 + 1]))).astype(BF16)
            cgb = cg.astype(BF16)
            cb = lax.dot_general(cgb, bg.astype(BF16), (((1,), (1,)), ((), ())), preferred_element_type=F32)
            state = st[g]
            y_off = jnp.dot(cgb, state.astype(BF16), preferred_element_type=F32)
            y_dg, st_new, ea, cdec = [], [], [], []
            for h in (h0, h1):
                ac = lanes(acum[:, h:h + 1])
                lmat = jnp.exp(jnp.where(causal, ac - ac.T, -jnp.inf))
                y_dg.append(jnp.dot((cb * lmat).astype(BF16), xdt, preferred_element_type=F32))
                a_last = ac[t - 1:t, :]
                bdec = bg * jnp.exp(a_last - ac)
                st_new.append(jnp.dot(bdec.T.astype(BF16), xdt, preferred_element_type=F32))
                ea.append(jnp.exp(ac))
                cdec.append(jnp.exp(a_last))
            y = (jnp.where(low, y_dg[0], y_dg[1]) + y_off * jnp.where(low, ea[0], ea[1])
                 + xg * dsk_ref[:, g * gl:(g + 1) * gl])
            st[g] = state * jnp.where(low, cdec[0], cdec[1]) + jnp.where(low, st_new[0], st_new[1])
            zg = z_ref[r0:r0 + t, g * gl:(g + 1) * gl]
            y = y * (zg * jax.nn.sigmoid(zg))
            o_ref[r0:r0 + t, g * gl:(g + 1) * gl] = _rms(y, ng_ref[:, g * gl:(g + 1) * gl]).astype(o_ref.dtype)


def _mamba2_ssd(z, xbc, dt, conv_w, conv_b, dt_bias, a_log, d_skip, norm_gain, chunks=8):
    bsz, s, w = z.shape
    ts = chunks * SSM_CHUNK
    row = lambda width: pl.BlockSpec((None, ts, width), lambda b, i: (b, i, 0))
    return pl.pallas_call(
        _ssd_kernel, grid=(bsz, s // ts),
        in_specs=[row(w), row(SSM_CONV_DIM), row(LANE), _full((CONV_WIDTH, SSM_CONV_DIM)),
                  _full((1, SSM_CONV_DIM)), _full((1, LANE)), _full((1, LANE)),
                  _full((1, w)), _full((1, w))],
        out_specs=row(w), out_shape=jax.ShapeDtypeStruct((bsz, s, w), BF16),
        scratch_shapes=[pltpu.VMEM((ts + CONV_PAD, SSM_CONV_DIM), F32),
                        pltpu.VMEM((SSM_GROUPS, SSM_STATE, GROUP_WIDTH // SSM_GROUPS), F32)],
        compiler_params=_params(2), name="ssd",
    )(z, xbc, dt, conv_w, conv_b, dt_bias, a_log, d_skip, norm_gain)


def _diff_kernel(tab_ref, qt_ref, k_ref, vt_ref, lq1, lk1, lq2, lk2, gain_ref, o_ref,
                 bias_scr, qm_scr, s_a, s_b, mc_a, mc_b, m_scr, acc_scr, ot_scr, *, lam_init):
    tq = DIFF_TQ
    qi = pl.program_id(1)
    n_sm = 2 * GROUP_HEADS
    grp = tq // SUBLANE
    acc_grp = DIFF_ACC_ROWS // SUBLANE
    s_bufs, mc_bufs = (s_a, s_b), (mc_a, mc_b)
    nbuf = len(s_bufs)

    @pl.when((pl.program_id(0) == 0) & (qi == 0))
    def _():
        ki_ = lax.broadcasted_iota(jnp.int32, (tq, tq), 0)
        qi_ = lax.broadcasted_iota(jnp.int32, (tq, tq), 1)
        for h in range(GROUP_HEADS):
            for d in range(DIFF_NEAR):
                dist = d * tq + qi_ - ki_
                bias = _bias_from_dist(jnp.maximum(dist, 0), tab_ref, GROUP_HEADS + h) * LOG2E
                bias_scr[d, h] = jnp.where(dist >= 0, bias, -jnp.inf)
            bias_scr[DIFF_NEAR, h] = jnp.full((tq, tq), tab_ref[NUM_BUCKETS - 1, GROUP_HEADS + h], F32) * LOG2E
            bias_scr[DIFF_NEAR + 1, h] = jnp.full((tq, tq), -jnp.inf, F32)

    qt = qt_ref[...]
    feat = lax.broadcasted_iota(jnp.int32, (GROUP_WIDTH, 1), 0)
    for idx in range(n_sm):
        qm_scr[idx] = jnp.where((feat // DIFF_QK_DIM) == idx, qt, jnp.zeros_like(qt))
    m_scr[...] = jnp.full(m_scr.shape, -1e30, F32)
    acc_scr[...] = jnp.zeros_like(acc_scr)

    c_far = [jnp.full((SUBLANE, tq), tab_ref[NUM_BUCKETS - 1, GROUP_HEADS + h], F32) * LOG2E
             for h in range(GROUP_HEADS)]
    ones_rows = jnp.ones((DIFF_ACC_ROWS - HEAD_DIM, tq), BF16)
    n_far = jnp.maximum(qi + 1 - DIFF_NEAR, 0)
    j_far = lax.div(jnp.maximum(n_far - 1, 0), 4 * nbuf)
    n_raw = 4 * nbuf * j_far

    def rows_max(x3):
        part = jnp.max(x3, axis=0)
        return jnp.broadcast_to(jnp.max(part, axis=0, keepdims=True), part.shape)

    def rows_sum(x3):
        part = jnp.sum(x3, axis=0)
        return jnp.broadcast_to(jnp.sum(part, axis=0, keepdims=True), part.shape)

    def key_block(ki):
        return k_ref[pl.ds(pl.multiple_of(jnp.minimum(ki, qi) * tq, tq), tq), :]

    def bias_tile(ki):
        d = qi - ki
        return jnp.where(d < 0, DIFF_NEAR + 1, jnp.minimum(d, DIFF_NEAR))

    def score_item(kb, idx, s_buf):
        s_buf[idx] = jnp.dot(kb, qm_scr[idx], preferred_element_type=F32)

    def max_any(tile, idx, s_buf, mc):
        t = s_buf[idx] + bias_scr[tile, idx // 2]
        s_buf[idx] = t
        mc[idx] = rows_max(t.reshape(grp, SUBLANE, tq))

    def max_far(tile, idx, s_buf, mc):
        mc[idx] = rows_max(s_buf[idx].reshape(grp, SUBLANE, tq)) + c_far[idx // 2]

    def exp_item(raw, vt, idx, s_buf, mc):
        h = idx // 2
        m_prev = m_scr[idx]
        m_next = jnp.maximum(m_prev, mc[idx])
        shift = m_next - jnp.where(raw, c_far[h], 0.0)
        p3 = jnp.exp2(s_buf[idx].reshape(grp, SUBLANE, tq) - shift[None])
        alpha = jnp.exp2(m_prev - m_next)
        lhs = jnp.concatenate([vt[h * HEAD_DIM:(h + 1) * HEAD_DIM, :], ones_rows], axis=0)
        pv = jnp.dot(lhs, p3.reshape(tq, tq).astype(BF16), preferred_element_type=F32)
        acc = acc_scr[idx].reshape(acc_grp, SUBLANE, tq) * alpha[None]
        acc_scr[idx] = acc.reshape(DIFF_ACC_ROWS, tq) + pv
        m_scr[idx] = m_next

    def trip(_, b0, max_item, nsub, refill=True):
        for u in range(nsub):
            b = b0 + u
            cur, nxt = u % nbuf, (u + 1) % nbuf
            kb = key_block(b + 2) if refill else None
            vt = vt_ref[jnp.minimum(b, qi)]
            raw = (b >= 1) & (b <= n_raw)
            tile = bias_tile(b + 1)
            for idx in range(n_sm):
                exp_item(raw, vt, idx, s_bufs[cur], mc_bufs[cur])
                if refill:
                    score_item(kb, idx, s_bufs[cur])
                if refill or u + 1 < nsub:
                    max_item(tile, idx, s_bufs[nxt], mc_bufs[nxt])
        return b0 + nsub

    for idx in range(n_sm):
        score_item(key_block(0), idx, s_bufs[0])
        score_item(key_block(1), idx, s_bufs[1])
    for idx in range(n_sm):
        max_any(bias_tile(0), idx, s_bufs[0], mc_bufs[0])
    long, short = 4 * nbuf, nbuf
    b0 = lax.fori_loop(0, j_far, functools.partial(trip, max_item=max_far, nsub=long), 0)
    n_even = short * lax.div(qi + short, short)
    left = n_even - short - b0
    n_long = lax.div(left, long)
    b0 = lax.fori_loop(0, n_long, functools.partial(trip, max_item=max_any, nsub=long), b0)
    n_short = lax.div(left - n_long * long, short)
    b0 = lax.fori_loop(0, n_short, functools.partial(trip, max_item=max_any, nsub=short), b0)
    trip(0, b0, max_item=max_any, nsub=short, refill=False)

    lam = (jnp.exp(jnp.sum(lq1[...] * lk1[...], axis=1, keepdims=True))
           - jnp.exp(jnp.sum(lq2[...] * lk2[...], axis=1, keepdims=True)) + lam_init)
    for h in range(GROUP_HEADS):
        shape3 = (HEAD_DIM // SUBLANE, SUBLANE, tq)
        a1, a2 = acc_scr[2 * h], acc_scr[2 * h + 1]
        o1 = a1[0:HEAD_DIM].reshape(shape3) / a1[HEAD_DIM:HEAD_DIM + SUBLANE][None]
        o2 = a2[0:HEAD_DIM].reshape(shape3) / a2[HEAD_DIM:HEAD_DIM + SUBLANE][None]
        oh = o1 - lam * o2
        ms = rows_sum(oh * oh) * (1.0 / HEAD_DIM)
        oh = oh * lax.rsqrt(ms + NORM_EPS)[None]
        ot_scr[h * HEAD_DIM:(h + 1) * HEAD_DIM, :] = oh.reshape(HEAD_DIM, tq)
    o_ref[...] = ((ot_scr[...].T * gain_ref[...]) * (1.0 - lam_init)).astype(o_ref.dtype)


def _diff_attention(qt, k, vt, rel_bias, lq1, lk1, lq2, lk2, gain, lam_init):
    bsz, s, gw = k.shape
    tq = DIFF_TQ
    n_sm = 2 * GROUP_HEADS
    lvec = _full((1, DIFF_QK_DIM))
    stat = pltpu.VMEM((n_sm, SUBLANE, tq), F32)
    return pl.pallas_call(
        functools.partial(_diff_kernel, lam_init=lam_init), grid=(bsz, s // tq),
        in_specs=[pl.BlockSpec(memory_space=pltpu.SMEM),
                  pl.BlockSpec((None, None, gw, tq), lambda b, i: (b, i, 0, 0)),
                  pl.BlockSpec((None, s, gw), lambda b, i: (b, 0, 0)),
                  pl.BlockSpec((None, s // tq, gw, tq), lambda b, i: (b, 0, 0, 0)),
                  lvec, lvec, lvec, lvec, _full((1, gw))],
        out_specs=pl.BlockSpec((None, tq, gw), lambda b, i: (b, i, 0)),
        out_shape=jax.ShapeDtypeStruct((bsz, s, gw), BF16),
        scratch_shapes=[pltpu.VMEM((DIFF_NEAR + 2, GROUP_HEADS, tq, tq), F32),
                        pltpu.VMEM((n_sm, gw, tq), BF16),
                        pltpu.VMEM((n_sm, tq, tq), F32), pltpu.VMEM((n_sm, tq, tq), F32),
                        stat, stat, stat,
                        pltpu.VMEM((n_sm, DIFF_ACC_ROWS, tq), F32),
                        pltpu.VMEM((gw, tq), F32)],
        compiler_params=_params(2), name="diff_attn",
    )(rel_bias, qt, k, vt, lq1, lk1, lq2, lk2, gain)


def _mix_ffn_kernel(h_ref, ya, yb, yc, yd, wo_ref, g_mix, g_pre, wu_ref, wd_ref, g_post, o_ref, *, chunk):
    gw = GROUP_WIDTH
    acc = jnp.dot(ya[...], wo_ref[0:gw, :], preferred_element_type=F32)
    for j, y in enumerate((yb, yc, yd), start=1):
        acc = acc + jnp.dot(y[...], wo_ref[j * gw:(j + 1) * gw, :], preferred_element_type=F32)
    x = h_ref[...] + _rms(acc, g_mix[...])
    u = _rms(x, g_pre[...]).astype(BF16)
    acc = jnp.zeros(x.shape, F32)
    for c in range(wu_ref.shape[1] // chunk):
        f = jnp.dot(u, wu_ref[:, c * chunk:(c + 1) * chunk], preferred_element_type=F32)
        f = jnp.square(jnp.maximum(f, 0.0)).astype(BF16)
        acc = acc + jnp.dot(f, wd_ref[c * chunk:(c + 1) * chunk, :], preferred_element_type=F32)
    o_ref[...] = x + _rms(acc, g_post[...])


def _mix_ffn(h, ya, yb, yc, yd, w_out, g_mix, g_pre, w_up, w_down, g_post, tm=512, chunk=1024):
    bsz, s, d = h.shape
    row = lambda width: pl.BlockSpec((None, tm, width), lambda b, i: (b, i, 0))
    vec = _full((1, d))
    return pl.pallas_call(
        functools.partial(_mix_ffn_kernel, chunk=chunk), grid=(bsz, s // tm),
        in_specs=[row(d)] + [row(GROUP_WIDTH)] * 4 + [_resident(w_out.shape), vec, vec,
                                                      _resident(w_up.shape), _resident(w_down.shape), vec],
        out_specs=row(d), out_shape=jax.ShapeDtypeStruct(h.shape, F32), compiler_params=_params(2),
        name="mix_ffn",
    )(h, ya, yb, yc, yd, w_out, g_mix, g_pre, w_up, w_down, g_post)


def _block_diag(w):
    nb, n, _ = w.shape
    eye = jnp.eye(nb, dtype=w.dtype)
    return (eye[:, None, :, None] * w[:, :, None, :]).reshape(nb * n, nb * n)


def _pad_lanes(v):
    return jnp.pad(v.astype(F32), (0, LANE - v.shape[0]))[None, :]


def kernel(x, rel_bias, norm_mix_pre, norm_mix_post, norm_ffn_pre, norm_ffn_post, w_in, w_out,
           lru_conv_w, lru_conv_b, lru_wa, lru_ba, lru_wx, lru_bx, lru_lambda,
           ssm_conv_w, ssm_conv_b, ssm_dt_bias, ssm_a_log, ssm_d, ssm_norm,
           diff_lq1, diff_lk1, diff_lq2, diff_lk2, diff_norm, w_ff_up, w_ff_down):
    depth = w_in.shape[0]
    vec = lambda p: p.astype(F32)[None, :]
    h = x
    for layer in range(depth):
        w_all = _pack_w_in(w_in, layer)
        qa, ka, va, gb, xb, zc, xbc, qdt, kd, vdt, dt = _in_proj(h, vec(norm_mix_pre[layer]), w_all)

        ya = _dilated_attention(qa, ka, va, rel_bias)
        yb = _rg_lru(gb, xb, lru_conv_w[layer], vec(lru_conv_b[layer]),
                     _block_diag(lru_wa[layer]).astype(BF16), vec(lru_ba[layer]),
                     _block_diag(lru_wx[layer]).astype(BF16), vec(lru_bx[layer]), vec(lru_lambda[layer]))
        yc = _mamba2_ssd(zc, xbc, dt, ssm_conv_w[layer], vec(ssm_conv_b[layer]),
                         _pad_lanes(ssm_dt_bias[layer]), _pad_lanes(ssm_a_log[layer]),
                         jnp.repeat(ssm_d[layer].astype(F32), HEAD_DIM)[None, :], vec(ssm_norm[layer]))
        lam_init = 0.8 - 0.6 * math.exp(-0.3 * layer)
        yd = _diff_attention(qdt, kd, vdt, rel_bias, vec(diff_lq1[layer]), vec(diff_lk1[layer]),
                             vec(diff_lq2[layer]), vec(diff_lk2[layer]),
                             jnp.tile(diff_norm[layer].astype(F32), GROUP_HEADS)[None, :], lam_init)

        h = _mix_ffn(h, ya, yb, yc, yd, w_out[layer].astype(BF16), vec(norm_mix_post[layer]),
                     vec(norm_ffn_pre[layer]), w_ff_up[layer].astype(BF16), w_ff_down[layer].astype(BF16),
                     vec(norm_ffn_post[layer]))
    return h
```

```python
import functools
import math

import numpy as np
import jax
import jax.numpy as jnp
from jax import lax
from jax.experimental import pallas as pl
from jax.experimental.pallas import tpu as pltpu

F32 = jnp.float32
BF16 = jnp.bfloat16
HIGHEST = lax.Precision.HIGHEST

NORM_EPS = 1e-6
HEAD_DIM = 64
GROUP_HEADS = 4
GROUP_WIDTH = GROUP_HEADS * HEAD_DIM
NUM_BUCKETS = 32
MAX_DISTANCE = 2048
DILATED_PATTERNS = ((128, 1), (512, 4), (2048, 16))
DIL_BLOCK = 128
DIL_TILE = DIL_BLOCK * max(d for _, d in DILATED_PATTERNS)
LRU_C = 8.0
CONV_WIDTH = 4
SSM_GROUPS = 2
SSM_STATE = 128
SSM_CHUNK = 128
SSM_CONV_DIM = GROUP_WIDTH + 2 * SSM_GROUPS * SSM_STATE
DIFF_QK_DIM = HEAD_DIM // 2
DIFF_TQ = 256
DIFF_ACC_ROWS = HEAD_DIM + 16
LOG2E = math.log2(math.e)
LN2 = math.log(2.0)
LANE = 128
SUBLANE = 8
CONV_PAD = 8
VMEM_LIMIT = 48 * 1024 * 1024

_C_QA, _C_KA, _C_VA = 0, 256, 512
_C_GB, _C_XB = 768, 1024
_C_ZC, _C_XBC = 1280, 1536
_C_QD, _C_KD, _C_VD = 2304, 2560, 2816
_C_DT = 3072
_C_END = 3200
_W_IN_DT = 3 * GROUP_WIDTH + 2 * GROUP_WIDTH + GROUP_WIDTH + SSM_CONV_DIM
SSM_HEADS = GROUP_HEADS


def _t5_thresholds():
    n = np.arange(1, 4 * MAX_DISTANCE)
    max_exact = NUM_BUCKETS // 2
    large = max_exact + (np.log(n / max_exact) / math.log(MAX_DISTANCE / max_exact)
                         * (NUM_BUCKETS - max_exact)).astype(np.int64)
    bucket = np.where(n < max_exact, n, np.minimum(large, NUM_BUCKETS - 1))
    return tuple(int(n[bucket >= b].min()) for b in range(1, NUM_BUCKETS))


_T5_THR = _t5_thresholds()
DIFF_NEAR = -(-(_T5_THR[-1] + DIFF_TQ - 1) // DIFF_TQ)


def _bias_from_dist(dist, tab_ref, col):
    out = jnp.full(dist.shape, tab_ref[0, col], F32)
    for b in range(1, NUM_BUCKETS):
        out = jnp.where(dist >= _T5_THR[b - 1], tab_ref[b, col], out)
    return out


def _rms(x, g):
    return x * lax.rsqrt(jnp.mean(x * x, axis=-1, keepdims=True) + NORM_EPS) * g


def _params(n_axes):
    return pltpu.CompilerParams(dimension_semantics=("arbitrary",) * n_axes,
                                vmem_limit_bytes=VMEM_LIMIT)


def _full(shape):
    return pl.BlockSpec(shape, lambda *_: (0,) * len(shape))


def _resident(shape):
    return pl.BlockSpec(shape, lambda *_: (0,) * len(shape), pipeline_mode=pl.Buffered(1))


def _pack_w_in_kernel(w_ref, o_ref):
    d_cols = _C_DT - _W_IN_DT
    o_ref[:, 0:_W_IN_DT] = w_ref[:, 0:_W_IN_DT].astype(BF16)
    o_ref[:, _W_IN_DT:_C_DT] = w_ref[:, _W_IN_DT + SSM_HEADS:_W_IN_DT + SSM_HEADS + d_cols].astype(BF16)
    lane = lax.broadcasted_iota(jnp.int32, (1, LANE), 1)
    dt_tile = w_ref[:, _W_IN_DT:_W_IN_DT + LANE]
    o_ref[:, _C_DT:_C_END] = jnp.where(lane < SSM_HEADS, dt_tile, 0.0).astype(BF16)


def _pack_w_in(w_in, layer, tr=256):
    depth, d, p_in = w_in.shape
    assert p_in == _C_DT + SSM_HEADS
    return pl.pallas_call(
        _pack_w_in_kernel, grid=(d // tr,),
        in_specs=[pl.BlockSpec((tr, p_in), lambda i: (layer * (d // tr) + i, 0))],
        out_specs=pl.BlockSpec((tr, _C_END), lambda i: (i, 0)),
        out_shape=jax.ShapeDtypeStruct((d, _C_END), BF16), compiler_params=_params(1), name="pack_w_in",
    )(w_in.reshape(depth * d, p_in))


def _in_proj_kernel(h_ref, g_ref, w_ref, qa, ka, va, gb, xb, zc, xbc, qdt, kd, vdt, dt):
    u = _rms(h_ref[...], g_ref[...]).astype(BF16)

    def seg(lo, hi):
        return jnp.dot(u, w_ref[:, lo:hi], preferred_element_type=F32)

    def store_transposed(ref, val):
        for j in range(ref.shape[0]):
            ref[j] = val[j * DIFF_TQ:(j + 1) * DIFF_TQ, :].T.astype(BF16)

    def store_halves(ref, val):
        for hh in range(ref.shape[0]):
            ref[hh] = val[:, hh * LANE:(hh + 1) * LANE]

    store_halves(qa, seg(_C_QA, _C_KA) * (HEAD_DIM ** -0.5 * LOG2E))
    store_halves(ka, seg(_C_KA, _C_VA))
    store_halves(va, seg(_C_VA, _C_GB))
    gb[...] = seg(_C_GB, _C_XB)
    xb[...] = seg(_C_XB, _C_ZC)
    zc[...] = seg(_C_ZC, _C_XBC)
    xbc[...] = seg(_C_XBC, _C_QD)
    store_transposed(qdt, seg(_C_QD, _C_KD) * (DIFF_QK_DIM ** -0.5 * LOG2E))
    kd[...] = seg(_C_KD, _C_VD).astype(BF16)
    store_transposed(vdt, seg(_C_VD, _C_DT))
    dt[...] = seg(_C_DT, _C_END)


def _in_proj(h, gain, w_all, tm=1024):
    bsz, s, d = h.shape
    gw = GROUP_WIDTH
    row = lambda width: pl.BlockSpec((None, tm, width), lambda b, i: (b, i, 0))
    shp = lambda width, dt: jax.ShapeDtypeStruct((bsz, s, width), dt)
    tshape = jax.ShapeDtypeStruct((bsz, s // DIFF_TQ, gw, DIFF_TQ), BF16)
    tspec = pl.BlockSpec((None, tm // DIFF_TQ, gw, DIFF_TQ), lambda b, i: (b, i, 0, 0))
    hshape = jax.ShapeDtypeStruct((bsz, gw // LANE, s, LANE), F32)
    hspec = pl.BlockSpec((None, gw // LANE, tm, LANE), lambda b, i: (b, 0, i, 0))
    out_shape = (hshape, hshape, hshape, shp(gw, F32), shp(gw, F32),
                 shp(gw, F32), shp(SSM_CONV_DIM, F32), tshape, shp(gw, BF16), tshape, shp(LANE, F32))
    out_specs = (hspec, hspec, hspec, row(gw), row(gw), row(gw), row(SSM_CONV_DIM),
                 tspec, row(gw), tspec, row(LANE))
    return pl.pallas_call(
        _in_proj_kernel, grid=(bsz, s // tm),
        in_specs=[row(d), _full((1, d)), _resident(w_all.shape)],
        out_specs=out_specs, out_shape=out_shape, compiler_params=_params(2), name="in_proj",
    )(h, gain, w_all)


def _dil_kernel(tab_ref, q_ref, k_ref, v_ref, out_ref, bias_scr, kbuf, vbuf, o_scr, lse_scr):
    tile = pl.program_id(1)
    tt = DIL_TILE
    nq, nk = DIL_BLOCK, 2 * DIL_BLOCK
    halves = GROUP_WIDTH // LANE
    heads_per_half = LANE // HEAD_DIM

    @pl.when((pl.program_id(0) == 0) & (tile == 0))
    def _():
        qi = lax.broadcasted_iota(jnp.int32, (nq, nk), 0)
        kj = lax.broadcasted_iota(jnp.int32, (nq, nk), 1)
        rel = qi + DIL_BLOCK - kj
        for p, (window, dil) in enumerate(DILATED_PATTERNS):
            valid = (rel >= 0) & (rel <= window // dil)
            dist = jnp.maximum(rel, 0) * dil
            for h in range(GROUP_HEADS):
                bias = _bias_from_dist(dist, tab_ref, h) * LOG2E
                bias_scr[0, p, h] = jnp.where(valid, bias, -jnp.inf)
                bias_scr[1, p, h] = jnp.where(valid & (kj >= DIL_BLOCK), bias, -jnp.inf)

    @pl.when(tile == 0)
    def _():
        kbuf[:, 0:tt, :] = jnp.zeros((halves, tt, LANE), F32)
        vbuf[:, 0:tt, :] = jnp.zeros((halves, tt, LANE), F32)

    kbuf[:, tt:2 * tt, :] = k_ref[...]
    vbuf[:, tt:2 * tt, :] = v_ref[...]

    head_in_half = lax.broadcasted_iota(jnp.int32, (1, LANE), 1) // HEAD_DIM

    for p, (_, dil) in enumerate(DILATED_PATTERNS):
        nbt = tt // (DIL_BLOCK * dil)

        def body(j, carry, p=p, dil=dil, nbt=nbt):
            r, n = j // nbt, j % nbt
            start = n * (DIL_BLOCK * dil) + r
            kstart = start + tt - DIL_BLOCK * dil
            if dil == 1:
                rows = pl.ds(pl.multiple_of(start, DIL_BLOCK), nq)
                krows = pl.ds(pl.multiple_of(kstart, DIL_BLOCK), nk)
            else:
                rows = pl.ds(start, nq, stride=dil)
                krows = pl.ds(kstart, nk, stride=dil)
            variant = jnp.where(tile * nbt + n > 0, 0, 1)
            scores = []
            for hh in range(halves):
                q = q_ref[hh, rows, :].astype(BF16)
                k = kbuf[hh, krows, :].astype(BF16)
                for hr in range(heads_per_half):
                    qh = jnp.where(head_in_half == hr, q, jnp.zeros_like(q))
                    scores.append(lax.dot_general(qh, k, (((1,), (1,)), ((), ())), preferred_element_type=F32))
            for hh in range(halves):
                v = vbuf[hh, krows, :].astype(BF16)
                o_acc = jnp.zeros((nq, LANE), F32)
                lse_acc = jnp.zeros((nq, LANE), F32)
                for hr in range(heads_per_half):
                    hm = head_in_half == hr
                    h = hh * heads_per_half + hr
                    sc = scores[h] + bias_scr[variant, p, h]
                    m = jnp.max(sc, axis=-1, keepdims=True)
                    e = jnp.exp2(sc - m)
                    den = jnp.sum(e, axis=-1, keepdims=True)
                    oh = jnp.dot(e.astype(BF16), v, preferred_element_type=F32) / den
                    o_acc = jnp.where(hm, oh, o_acc)
                    lse_acc = jnp.where(hm, m * LN2 + jnp.log(den), lse_acc)
                o_scr[p, hh, rows, :] = o_acc
                lse_scr[p, hh, rows, :] = lse_acc
            return carry

        lax.fori_loop(0, tt // DIL_BLOCK, body, 0, unroll=8)

    cm = 256

    def combine(c, carry):
        rows = pl.ds(pl.multiple_of(c * cm, cm), cm)
        for hh in range(halves):
            l0, l1, l2 = lse_scr[0, hh, rows, :], lse_scr[1, hh, rows, :], lse_scr[2, hh, rows, :]
            m = jnp.maximum(jnp.maximum(l0, l1), l2)
            w0, w1, w2 = jnp.exp(l0 - m), jnp.exp(l1 - m), jnp.exp(l2 - m)
            num = w0 * o_scr[0, hh, rows, :] + w1 * o_scr[1, hh, rows, :] + w2 * o_scr[2, hh, rows, :]
            out_ref[rows, hh * LANE:(hh + 1) * LANE] = (num / (w0 + w1 + w2)).astype(out_ref.dtype)
        return carry

    lax.fori_loop(0, tt // cm, combine, 0)
    kbuf[:, 0:tt, :] = k_ref[...]
    vbuf[:, 0:tt, :] = v_ref[...]


def _dilated_attention(q, k, v, rel_bias):
    bsz, halves, s, _ = q.shape
    tt = DIL_TILE
    n_pat = len(DILATED_PATTERNS)
    assert s % tt == 0 and n_pat == 3
    hspec = pl.BlockSpec((None, halves, tt, LANE), lambda b, i: (b, 0, i, 0))
    return pl.pallas_call(
        _dil_kernel, grid=(bsz, s // tt),
        in_specs=[pl.BlockSpec(memory_space=pltpu.SMEM), hspec, hspec, hspec],
        out_specs=pl.BlockSpec((None, tt, GROUP_WIDTH), lambda b, i: (b, i, 0)),
        out_shape=jax.ShapeDtypeStruct((bsz, s, GROUP_WIDTH), BF16),
        scratch_shapes=[pltpu.VMEM((2, n_pat, GROUP_HEADS, DIL_BLOCK, 2 * DIL_BLOCK), F32),
                        pltpu.VMEM((halves, 2 * tt, LANE), F32), pltpu.VMEM((halves, 2 * tt, LANE), F32),
                        pltpu.VMEM((n_pat, halves, tt, LANE), F32),
                        pltpu.VMEM((n_pat, halves, tt, LANE), F32)],
        compiler_params=_params(2), name="dilated_attn",
    )(rel_bias, q, k, v)


def _causal_conv(x, xbuf, cw_ref, cb_ref, first_tile):
    t = x.shape[0]

    @pl.when(first_tile)
    def _():
        xbuf[0:CONV_PAD, :] = jnp.zeros((CONV_PAD, x.shape[1]), F32)

    xbuf[CONV_PAD:CONV_PAD + t, :] = x
    y = cb_ref[...] + cw_ref[CONV_WIDTH - 1:CONV_WIDTH, :] * x
    for kk in range(CONV_WIDTH - 1):
        off = CONV_PAD - (CONV_WIDTH - 1) + kk
        y = y + cw_ref[kk:kk + 1, :] * xbuf[off:off + t, :]
    xbuf[0:CONV_PAD, :] = x[t - CONV_PAD:t, :]
    return y


def _lru_kernel(g_ref, x_ref, cw_ref, cb_ref, wa_ref, ba_ref, wx_ref, bx_ref, lam_ref, o_ref,
                xbuf, a_scr, b_scr, h_scr, hcar):
    first_tile = pl.program_id(1) == 0
    ts = x_ref.shape[0]

    @pl.when(first_tile)
    def _():
        hcar[...] = jnp.zeros_like(hcar)

    xc = _causal_conv(x_ref[...], xbuf, cw_ref, cb_ref, first_tile)
    xcb = xc.astype(BF16)
    r = jax.nn.sigmoid(jnp.dot(xcb, wa_ref[...], preferred_element_type=F32) + ba_ref[...])
    i = jax.nn.sigmoid(jnp.dot(xcb, wx_ref[...], preferred_element_type=F32) + bx_ref[...])
    neg_lam = -lam_ref[...]
    softplus = jnp.maximum(neg_lam, 0.0) + jnp.log1p(jnp.exp(-jnp.abs(neg_lam)))
    log_a = -LRU_C * r * softplus
    a = jnp.exp(log_a)
    a_scr[...] = a
    b_scr[...] = jnp.sqrt(-jnp.tanh(log_a) * (a * a + 1.0)) * (i * xc)

    row = lax.broadcasted_iota(jnp.int32, (SUBLANE, GROUP_WIDTH), 0)

    def body(j, hprev):
        r0 = pl.multiple_of(j * SUBLANE, SUBLANE)
        a = a_scr[pl.ds(r0, SUBLANE), :]
        b = b_scr[pl.ds(r0, SUBLANE), :]
        for d in (1, 2, 4):
            a_sh = jnp.where(row >= d, pltpu.roll(a, d, 0), 1.0)
            b_sh = jnp.where(row >= d, pltpu.roll(b, d, 0), 0.0)
            b = a * b_sh + b
            a = a * a_sh
        h_scr[pl.ds(r0, SUBLANE), :] = a * hprev + b
        a_last = jnp.broadcast_to(a[SUBLANE - 1:SUBLANE, :], a.shape)
        b_last = jnp.broadcast_to(b[SUBLANE - 1:SUBLANE, :], b.shape)
        return a_last * hprev + b_last

    hcar[...] = lax.fori_loop(0, ts // SUBLANE, body, hcar[...], unroll=8)
    o_ref[...] = (jax.nn.gelu(g_ref[...], approximate=True) * h_scr[...]).astype(o_ref.dtype)


def _rg_lru(gate, x, conv_w, conv_b, wa, ba, wx, bx, lam, ts=512):
    bsz, s, w = x.shape
    row = pl.BlockSpec((None, ts, w), lambda b, i: (b, i, 0))
    vec = _full((1, w))
    return pl.pallas_call(
        _lru_kernel, grid=(bsz, s // ts),
        in_specs=[row, row, _full((CONV_WIDTH, w)), vec, _full((w, w)), vec, _full((w, w)), vec, vec],
        out_specs=row, out_shape=jax.ShapeDtypeStruct((bsz, s, w), BF16),
        scratch_shapes=[pltpu.VMEM((ts + CONV_PAD, w), F32), pltpu.VMEM((ts, w), F32),
                        pltpu.VMEM((ts, w), F32), pltpu.VMEM((ts, w), F32), pltpu.VMEM((SUBLANE, w), F32)],
        compiler_params=_params(2), name="rg_lru",
    )(gate, x, conv_w, conv_b, wa, ba, wx, bx, lam)


def _ssd_kernel(z_ref, xbc_ref, dt_ref, cw_ref, cb_ref, dtb_ref, alog_ref, dsk_ref, ng_ref, o_ref,
                xbuf, st):
    first_tile = pl.program_id(1) == 0
    t = SSM_CHUNK
    gl = GROUP_WIDTH // SSM_GROUPS

    @pl.when(first_tile)
    def _():
        st[...] = jnp.zeros_like(st)

    xc_all = _causal_conv(xbc_ref[...], xbuf, cw_ref, cb_ref, first_tile)
    xc_all = xc_all * jax.nn.sigmoid(xc_all)
    dt_in = dt_ref[...] + dtb_ref[...]
    dtl_all = jnp.maximum(dt_in, 0.0) + jnp.log1p(jnp.exp(-jnp.abs(dt_in)))
    adt_all = dtl_all * (-jnp.exp(alog_ref[...]))
    li = lax.broadcasted_iota(jnp.int32, (t, t), 0)
    si = lax.broadcasted_iota(jnp.int32, (t, t), 1)
    causal = li >= si
    tri = causal.astype(F32)
    low = lax.broadcasted_iota(jnp.int32, (1, gl), 1) < HEAD_DIM

    def lanes(col):
        return jnp.broadcast_to(col, (t, LANE))

    for cc in range(xbc_ref.shape[0] // t):
        r0 = cc * t
        xc = xc_all[r0:r0 + t]
        dtl = dtl_all[r0:r0 + t]
        acum = jnp.dot(tri, adt_all[r0:r0 + t], precision=HIGHEST, preferred_element_type=F32)
        for g in range(SSM_GROUPS):
            xg = xc[:, g * gl:(g + 1) * gl]
            bg = xc[:, GROUP_WIDTH + g * SSM_STATE:GROUP_WIDTH + (g + 1) * SSM_STATE]
            cg = xc[:, GROUP_WIDTH + (SSM_GROUPS + g) * SSM_STATE:GROUP_WIDTH + (SSM_GROUPS + g + 1) * SSM_STATE]
            h0, h1 = 2 * g, 2 * g + 1
            xdt = (xg * jnp.where(low, lanes(dtl[:, h0:h0 + 1]), lanes(dtl[:, h1:h1 + 1]))).astype(BF16)
            cgb = cg.astype(BF16)
            cb = lax.dot_general(cgb, bg.astype(BF16), (((1,), (1,)), ((), ())), preferred_element_type=F32)
            state = st[g]
            y_off = jnp.dot(cgb, state.astype(BF16), preferred_element_type=F32)
            y_dg, st_new, ea, cdec = [], [], [], []
            for h in (h0, h1):
                ac = lanes(acum[:, h:h + 1])
                lmat = jnp.exp(jnp.where(causal, ac - ac.T, -jnp.inf))
                y_dg.append(jnp.dot((cb * lmat).astype(BF16), xdt, preferred_element_type=F32))
                a_last = ac[t - 1:t, :]
                bdec = bg * jnp.exp(a_last - ac)
                st_new.append(jnp.dot(bdec.T.astype(BF16), xdt, preferred_element_type=F32))
                ea.append(jnp.exp(ac))
                cdec.append(jnp.exp(a_last))
            y = (jnp.where(low, y_dg[0], y_dg[1]) + y_off * jnp.where(low, ea[0], ea[1])
                 + xg * dsk_ref[:, g * gl:(g + 1) * gl])
            st[g] = state * jnp.where(low, cdec[0], cdec[1]) + jnp.where(low, st_new[0], st_new[1])
            zg = z_ref[r0:r0 + t, g * gl:(g + 1) * gl]
            y = y * (zg * jax.nn.sigmoid(zg))
            o_ref[r0:r0 + t, g * gl:(g + 1) * gl] = _rms(y, ng_ref[:, g * gl:(g + 1) * gl]).astype(o_ref.dtype)


def _mamba2_ssd(z, xbc, dt, conv_w, conv_b, dt_bias, a_log, d_skip, norm_gain, chunks=8):
    bsz, s, w = z.shape
    ts = chunks * SSM_CHUNK
    row = lambda width: pl.BlockSpec((None, ts, width), lambda b, i: (b, i, 0))
    return pl.pallas_call(
        _ssd_kernel, grid=(bsz, s // ts),
        in_specs=[row(w), row(SSM_CONV_DIM), row(LANE), _full((CONV_WIDTH, SSM_CONV_DIM)),
                  _full((1, SSM_CONV_DIM)), _full((1, LANE)), _full((1, LANE)),
                  _full((1, w)), _full((1, w))],
        out_specs=row(w), out_shape=jax.ShapeDtypeStruct((bsz, s, w), BF16),
        scratch_shapes=[pltpu.VMEM((ts + CONV_PAD, SSM_CONV_DIM), F32),
                        pltpu.VMEM((SSM_GROUPS, SSM_STATE, GROUP_WIDTH // SSM_GROUPS), F32)],
        compiler_params=_params(2), name="ssd",
    )(z, xbc, dt, conv_w, conv_b, dt_bias, a_log, d_skip, norm_gain)


def _lru_ssd_kernel(g_ref, x_ref, l_cw, l_cb, wa, ba, wx, bx, lam, z_ref, xbc_ref, dt_ref, s_cw, s_cb, dtb, alog, dsk, ng,
                    yb_ref, yc_ref, l_xbuf, a_scr, b_scr, h_scr, hcar, s_xbuf, st):
    _lru_kernel(g_ref, x_ref, l_cw, l_cb, wa, ba, wx, bx, lam, yb_ref, l_xbuf, a_scr, b_scr, h_scr, hcar)
    _ssd_kernel(z_ref, xbc_ref, dt_ref, s_cw, s_cb, dtb, alog, dsk, ng, yc_ref, s_xbuf, st)


def _lru_ssd(gate, x, lru_params, z, xbc, dt, ssd_params, ts=1024):
    bsz, s, w = x.shape
    assert ts % SSM_CHUNK == 0
    row = lambda width: pl.BlockSpec((None, ts, width), lambda b, i: (b, i, 0))
    vec = lambda width: _full((1, width))
    out = jax.ShapeDtypeStruct((bsz, s, w), BF16)
    return pl.pallas_call(
        _lru_ssd_kernel, grid=(bsz, s // ts),
        in_specs=[row(w), row(w), _full((CONV_WIDTH, w)), vec(w), _full((w, w)), vec(w), _full((w, w)), vec(w), vec(w),
                  row(w), row(SSM_CONV_DIM), row(LANE), _full((CONV_WIDTH, SSM_CONV_DIM)), vec(SSM_CONV_DIM),
                  vec(LANE), vec(LANE), vec(w), vec(w)],
        out_specs=(row(w), row(w)), out_shape=(out, out),
        scratch_shapes=[pltpu.VMEM((ts + CONV_PAD, w), F32), pltpu.VMEM((ts, w), F32), pltpu.VMEM((ts, w), F32),
                        pltpu.VMEM((ts, w), F32), pltpu.VMEM((SUBLANE, w), F32),
                        pltpu.VMEM((ts + CONV_PAD, SSM_CONV_DIM), F32),
                        pltpu.VMEM((SSM_GROUPS, SSM_STATE, w // SSM_GROUPS), F32)],
        compiler_params=_params(2), name="lru_ssd",
    )(gate, x, *lru_params, z, xbc, dt, *ssd_params)


def _diff_kernel(tab_ref, qt_ref, k_ref, vt_ref, lq1, lk1, lq2, lk2, gain_ref, o_ref,
                 bias_scr, qm_scr, s_a, s_b, mc_a, mc_b, m_scr, acc_scr, ot_scr, *, lam_init):
    tq = DIFF_TQ
    qi = pl.program_id(1)
    n_sm = 2 * GROUP_HEADS
    grp = tq // SUBLANE
    acc_grp = DIFF_ACC_ROWS // SUBLANE
    s_bufs, mc_bufs = (s_a, s_b), (mc_a, mc_b)
    nbuf = len(s_bufs)

    @pl.when((pl.program_id(0) == 0) & (qi == 0))
    def _():
        ki_ = lax.broadcasted_iota(jnp.int32, (tq, tq), 0)
        qi_ = lax.broadcasted_iota(jnp.int32, (tq, tq), 1)
        for h in range(GROUP_HEADS):
            for d in range(DIFF_NEAR):
                dist = d * tq + qi_ - ki_
                bias = _bias_from_dist(jnp.maximum(dist, 0), tab_ref, GROUP_HEADS + h) * LOG2E
                bias_scr[d, h] = jnp.where(dist >= 0, bias, -jnp.inf)
            bias_scr[DIFF_NEAR, h] = jnp.full((tq, tq), tab_ref[NUM_BUCKETS - 1, GROUP_HEADS + h], F32) * LOG2E
            bias_scr[DIFF_NEAR + 1, h] = jnp.full((tq, tq), -jnp.inf, F32)

    qt = qt_ref[...]
    feat = lax.broadcasted_iota(jnp.int32, (GROUP_WIDTH, 1), 0)
    for idx in range(n_sm):
        qm_scr[idx] = jnp.where((feat // DIFF_QK_DIM) == idx, qt, jnp.zeros_like(qt))
    m_scr[...] = jnp.full(m_scr.shape, -1e30, F32)
    acc_scr[...] = jnp.zeros_like(acc_scr)

    c_far = [jnp.full((SUBLANE, tq), tab_ref[NUM_BUCKETS - 1, GROUP_HEADS + h], F32) * LOG2E
             for h in range(GROUP_HEADS)]
    ones_rows = jnp.ones((DIFF_ACC_ROWS - HEAD_DIM, tq), BF16)
    n_far = jnp.maximum(qi + 1 - DIFF_NEAR, 0)
    j_far = lax.div(jnp.maximum(n_far - 1, 0), 4 * nbuf)
    n_raw = 4 * nbuf * j_far

    def rows_max(x3):
        part = jnp.max(x3, axis=0)
        return jnp.broadcast_to(jnp.max(part, axis=0, keepdims=True), part.shape)

    def rows_sum(x3):
        part = jnp.sum(x3, axis=0)
        return jnp.broadcast_to(jnp.sum(part, axis=0, keepdims=True), part.shape)

    def key_block(ki):
        return k_ref[pl.ds(pl.multiple_of(jnp.minimum(ki, qi) * tq, tq), tq), :]

    def bias_tile(ki):
        d = qi - ki
        return jnp.where(d < 0, DIFF_NEAR + 1, jnp.minimum(d, DIFF_NEAR))

    def score_item(kb, idx, s_buf):
        s_buf[idx] = jnp.dot(kb, qm_scr[idx], preferred_element_type=F32)

    def max_any(tile, idx, s_buf, mc):
        t = s_buf[idx] + bias_scr[tile, idx // 2]
        s_buf[idx] = t
        mc[idx] = rows_max(t.reshape(grp, SUBLANE, tq))

    def max_far(tile, idx, s_buf, mc):
        mc[idx] = rows_max(s_buf[idx].reshape(grp, SUBLANE, tq)) + c_far[idx // 2]

    def exp_item(raw, vt, idx, s_buf, mc):
        h = idx // 2
        m_prev = m_scr[idx]
        m_next = jnp.maximum(m_prev, mc[idx])
        shift = m_next - jnp.where(raw, c_far[h], 0.0)
        p3 = jnp.exp2(s_buf[idx].reshape(grp, SUBLANE, tq) - shift[None])
        alpha = jnp.exp2(m_prev - m_next)
        lhs = jnp.concatenate([vt[h * HEAD_DIM:(h + 1) * HEAD_DIM, :], ones_rows], axis=0)
        pv = jnp.dot(lhs, p3.reshape(tq, tq).astype(BF16), preferred_element_type=F32)
        acc = acc_scr[idx].reshape(acc_grp, SUBLANE, tq) * alpha[None]
        acc_scr[idx] = acc.reshape(DIFF_ACC_ROWS, tq) + pv
        m_scr[idx] = m_next

    def trip(_, b0, max_item, nsub, refill=True):
        for u in range(nsub):
            b = b0 + u
            cur, nxt = u % nbuf, (u + 1) % nbuf
            kb = key_block(b + 2) if refill else None
            vt = vt_ref[jnp.minimum(b, qi)]
            raw = (b >= 1) & (b <= n_raw)
            tile = bias_tile(b + 1)
            for idx in range(n_sm):
                exp_item(raw, vt, idx, s_bufs[cur], mc_bufs[cur])
                if refill:
                    score_item(kb, idx, s_bufs[cur])
                if refill or u + 1 < nsub:
                    max_item(tile, idx, s_bufs[nxt], mc_bufs[nxt])
        return b0 + nsub

    for idx in range(n_sm):
        score_item(key_block(0), idx, s_bufs[0])
        score_item(key_block(1), idx, s_bufs[1])
    for idx in range(n_sm):
        max_any(bias_tile(0), idx, s_bufs[0], mc_bufs[0])
    long, short = 4 * nbuf, nbuf
    b0 = lax.fori_loop(0, j_far, functools.partial(trip, max_item=max_far, nsub=long), 0)
    n_even = short * lax.div(qi + short, short)
    left = n_even - short - b0
    n_long = lax.div(left, long)
    b0 = lax.fori_loop(0, n_long, functools.partial(trip, max_item=max_any, nsub=long), b0)
    n_short = lax.div(left - n_long * long, short)
    b0 = lax.fori_loop(0, n_short, functools.partial(trip, max_item=max_any, nsub=short), b0)
    trip(0, b0, max_item=max_any, nsub=short, refill=False)

    lam = (jnp.exp(jnp.sum(lq1[...] * lk1[...], axis=1, keepdims=True))
           - jnp.exp(jnp.sum(lq2[...] * lk2[...], axis=1, keepdims=True)) + lam_init)
    for h in range(GROUP_HEADS):
        shape3 = (HEAD_DIM // SUBLANE, SUBLANE, tq)
        a1, a2 = acc_scr[2 * h], acc_scr[2 * h + 1]
        o1 = a1[0:HEAD_DIM].reshape(shape3) / a1[HEAD_DIM:HEAD_DIM + SUBLANE][None]
        o2 = a2[0:HEAD_DIM].reshape(shape3) / a2[HEAD_DIM:HEAD_DIM + SUBLANE][None]
        oh = o1 - lam * o2
        ms = rows_sum(oh * oh) * (1.0 / HEAD_DIM)
        oh = oh * lax.rsqrt(ms + NORM_EPS)[None]
        ot_scr[h * HEAD_DIM:(h + 1) * HEAD_DIM, :] = oh.reshape(HEAD_DIM, tq)
    o_ref[...] = ((ot_scr[...].T * gain_ref[...]) * (1.0 - lam_init)).astype(o_ref.dtype)


def _diff_attention(qt, k, vt, rel_bias, lq1, lk1, lq2, lk2, gain, lam_init):
    bsz, s, gw = k.shape
    tq = DIFF_TQ
    n_sm = 2 * GROUP_HEADS
    lvec = _full((1, DIFF_QK_DIM))
    stat = pltpu.VMEM((n_sm, SUBLANE, tq), F32)
    return pl.pallas_call(
        functools.partial(_diff_kernel, lam_init=lam_init), grid=(bsz, s // tq),
        in_specs=[pl.BlockSpec(memory_space=pltpu.SMEM),
                  pl.BlockSpec((None, None, gw, tq), lambda b, i: (b, i, 0, 0)),
                  pl.BlockSpec((None, s, gw), lambda b, i: (b, 0, 0)),
                  pl.BlockSpec((None, s // tq, gw, tq), lambda b, i: (b, 0, 0, 0)),
                  lvec, lvec, lvec, lvec, _full((1, gw))],
        out_specs=pl.BlockSpec((None, tq, gw), lambda b, i: (b, i, 0)),
        out_shape=jax.ShapeDtypeStruct((bsz, s, gw), BF16),
        scratch_shapes=[pltpu.VMEM((DIFF_NEAR + 2, GROUP_HEADS, tq, tq), F32),
                        pltpu.VMEM((n_sm, gw, tq), BF16),
                        pltpu.VMEM((n_sm, tq, tq), F32), pltpu.VMEM((n_sm, tq, tq), F32),
                        stat, stat, stat,
                        pltpu.VMEM((n_sm, DIFF_ACC_ROWS, tq), F32),
                        pltpu.VMEM((gw, tq), F32)],
        compiler_params=_params(2), name="diff_attn",
    )(rel_bias, qt, k, vt, lq1, lk1, lq2, lk2, gain)


def _mix_ffn_kernel(h_ref, ya, yb, yc, yd, wo_ref, g_mix, g_pre, wu_ref, wd_ref, g_post, o_ref, *, chunk):
    gw = GROUP_WIDTH
    acc = jnp.dot(ya[...], wo_ref[0:gw, :], preferred_element_type=F32)
    for j, y in enumerate((yb, yc, yd), start=1):
        acc = acc + jnp.dot(y[...], wo_ref[j * gw:(j + 1) * gw, :], preferred_element_type=F32)
    x = h_ref[...] + _rms(acc, g_mix[...])
    u = _rms(x, g_pre[...]).astype(BF16)
    acc = jnp.zeros(x.shape, F32)
    for c in range(wu_ref.shape[1] // chunk):
        f = jnp.dot(u, wu_ref[:, c * chunk:(c + 1) * chunk], preferred_element_type=F32)
        f = jnp.square(jnp.maximum(f, 0.0)).astype(BF16)
        acc = acc + jnp.dot(f, wd_ref[c * chunk:(c + 1) * chunk, :], preferred_element_type=F32)
    o_ref[...] = x + _rms(acc, g_post[...])


def _mix_ffn(h, ya, yb, yc, yd, w_out, g_mix, g_pre, w_up, w_down, g_post, tm=512, chunk=1024):
    bsz, s, d = h.shape
    row = lambda width: pl.BlockSpec((None, tm, width), lambda b, i: (b, i, 0))
    vec = _full((1, d))
    return pl.pallas_call(
        functools.partial(_mix_ffn_kernel, chunk=chunk), grid=(bsz, s // tm),
        in_specs=[row(d)] + [row(GROUP_WIDTH)] * 4 + [_resident(w_out.shape), vec, vec,
                                                      _resident(w_up.shape), _resident(w_down.shape), vec],
        out_specs=row(d), out_shape=jax.ShapeDtypeStruct(h.shape, F32), compiler_params=_params(2),
        name="mix_ffn",
    )(h, ya, yb, yc, yd, w_out, g_mix, g_pre, w_up, w_down, g_post)


def _block_diag(w):
    nb, n, _ = w.shape
    eye = jnp.eye(nb, dtype=w.dtype)
    return (eye[:, None, :, None] * w[:, :, None, :]).reshape(nb * n, nb * n)


def _pad_lanes(v):
    return jnp.pad(v.astype(F32), (0, LANE - v.shape[0]))[None, :]


def kernel(x, rel_bias, norm_mix_pre, norm_mix_post, norm_ffn_pre, norm_ffn_post, w_in, w_out,
           lru_conv_w, lru_conv_b, lru_wa, lru_ba, lru_wx, lru_bx, lru_lambda,
           ssm_conv_w, ssm_conv_b, ssm_dt_bias, ssm_a_log, ssm_d, ssm_norm,
           diff_lq1, diff_lk1, diff_lq2, diff_lk2, diff_norm, w_ff_up, w_ff_down):
    depth = w_in.shape[0]
    vec = lambda p: p.astype(F32)[None, :]
    h = x
    for layer in range(depth):
        w_all = _pack_w_in(w_in, layer)
        qa, ka, va, gb, xb, zc, xbc, qdt, kd, vdt, dt = _in_proj(h, vec(norm_mix_pre[layer]), w_all)

        ya = _dilated_attention(qa, ka, va, rel_bias)
        lru_params = (lru_conv_w[layer], vec(lru_conv_b[layer]),
                      _block_diag(lru_wa[layer]).astype(BF16), vec(lru_ba[layer]),
                      _block_diag(lru_wx[layer]).astype(BF16), vec(lru_bx[layer]), vec(lru_lambda[layer]))
        ssd_params = (ssm_conv_w[layer], vec(ssm_conv_b[layer]),
                      _pad_lanes(ssm_dt_bias[layer]), _pad_lanes(ssm_a_log[layer]),
                      jnp.repeat(ssm_d[layer].astype(F32), HEAD_DIM)[None, :], vec(ssm_norm[layer]))
        yb, yc = _lru_ssd(gb, xb, lru_params, zc, xbc, dt, ssd_params)
        lam_init = 0.8 - 0.6 * math.exp(-0.3 * layer)
        yd = _diff_attention(qdt, kd, vdt, rel_bias, vec(diff_lq1[layer]), vec(diff_lk1[layer]),
                             vec(diff_lq2[layer]), vec(diff_lk2[layer]),
                             jnp.tile(diff_norm[layer].astype(F32), GROUP_HEADS)[None, :], lam_init)

        h = _mix_ffn(h, ya, yb, yc, yd, w_out[layer].astype(BF16), vec(norm_mix_post[layer]),
                     vec(norm_ffn_pre[layer]), w_ff_up[layer].astype(BF16), w_ff_down[layer].astype(BF16),
                     vec(norm_ffn_post[layer]))
    return h
```
